```python
import jax, jax.numpy as jnp
from jax import lax
import numpy as np

D_MODEL = 2048
BATCH = 4
SEQ = 2048
DEPTH = 4

GRID_W = 64
CTX_LEN = 256
N_MIXERS = 3
EPS = 1e-6
NEG_INF = -1e30

D_FF = ((8 * D_MODEL // 3 + 255) // 256) * 256

GLA_HEADS = 4
GLA_DK = D_MODEL // 2
GLA_DV = D_MODEL
GLA_DK_HEAD = GLA_DK // GLA_HEADS
GLA_DV_HEAD = GLA_DV // GLA_HEADS
GLA_RANK = 16
GLA_TAU = 16.0
GLA_CHUNK = 64
GLA_IN = 2 * GLA_DK + 2 * GLA_DV + 2 * GLA_RANK

SWA_HEAD_DIM = 64
SWA_HEADS = D_MODEL // SWA_HEAD_DIM
SWA_KV_HEADS = 4
SWA_GROUP = SWA_HEADS // SWA_KV_HEADS
SWA_WINDOW = 128
SWA_BLOCK = 128
SWA_IN = (SWA_HEADS + 2 * SWA_KV_HEADS) * SWA_HEAD_DIM
ROPE_BASE = 10000.0

GMLP_WIDTH = D_MODEL
GMLP_CHUNK = 128
GMLP_GROUPS = 16
GMLP_GROUP_DIM = GMLP_WIDTH // GMLP_GROUPS

N_A = (DEPTH + 2) // 3
N_B = (DEPTH + 1) // 3
N_C = DEPTH // 3

kernel_name = "hybrid_gla_swa_gmlp_diffusion_trunk"

F32 = jnp.float32


def rmsnorm(x, g):
    xf = x.astype(F32)
    y = xf * lax.rsqrt(jnp.mean(xf * xf, axis=-1, keepdims=True) + EPS)
    return (y * g.astype(F32)).astype(x.dtype)


def layernorm(x, g, b):
    xf = x.astype(F32)
    mu = jnp.mean(xf, axis=-1, keepdims=True)
    var = jnp.mean(jnp.square(xf - mu), axis=-1, keepdims=True)
    return ((xf - mu) * lax.rsqrt(var + EPS) * g.astype(F32) + b.astype(F32)).astype(x.dtype)


def swiglu(h, w1, w3, w2):
    return (jax.nn.silu(h @ w1) * (h @ w3)) @ w2


def axial_rope_tables(n_tokens):
    rows = n_tokens // GRID_W
    quarter = SWA_HEAD_DIM // 4
    inv_freq = ROPE_BASE ** (-jnp.arange(quarter, dtype=F32) / quarter)
    row = jnp.repeat(jnp.arange(rows), GRID_W).astype(F32)
    col = jnp.tile(jnp.arange(GRID_W), rows).astype(F32)
    ang_r = row[:, None] * inv_freq
    ang_c = col[:, None] * inv_freq
    ang = jnp.concatenate([ang_r, ang_r, ang_c, ang_c], axis=-1)
    return jnp.cos(ang), jnp.sin(ang)


def apply_axial_rope(x, cos, sin):
    half, quarter = SWA_HEAD_DIM // 2, SWA_HEAD_DIM // 4

    def rot_half(a):
        return jnp.concatenate([-a[..., quarter:], a[..., :quarter]], axis=-1)

    rot = jnp.concatenate([rot_half(x[..., :half]), rot_half(x[..., half:])], axis=-1)
    return (x * cos + rot * sin).astype(x.dtype)


def gla_chunk_scan(q, k, v, g, s0):
    B, H, T, DK = q.shape
    n = T // GLA_CHUNK

    def split(a):
        return a.reshape(B, H, n, GLA_CHUNK, a.shape[-1]).transpose(2, 0, 1, 3, 4)

    mask = jnp.tril(jnp.ones((GLA_CHUNK, GLA_CHUNK), bool))[:, :, None]

    def step(s, inp):
        qc, kc, vc, gc = inp
        b = jnp.cumsum(gc.astype(F32), axis=2)
        o_inter = jnp.einsum('bhik,bhkv->bhiv', qc * jnp.exp(b), s)
        diff = b[:, :, :, None, :] - b[:, :, None, :, :]
        decay = jnp.where(mask, jnp.exp(jnp.where(mask, diff, 0.0)), 0.0)
        att = jnp.einsum('bhik,bhjk,bhijk->bhij', qc, kc, decay)
        o_intra = jnp.einsum('bhij,bhjv->bhiv', att, vc)
        b_last = b[:, :, -1:, :]
        s_new = (jnp.exp(b_last[:, :, 0, :])[..., None] * s
                 + jnp.einsum('bhjk,bhjv->bhkv', kc * jnp.exp(b_last - b), vc))
        return s_new, o_inter + o_intra

    s_fin, o = lax.scan(step, s0, (split(q), split(k), split(v), split(g)))
    return o.transpose(1, 2, 0, 3, 4).reshape(B, H, T, v.shape[-1]), s_fin


def gla_mixer(hx, hc, w_in, wa2, ba, onorm_g, wo, need_ctx):
    B, L, _ = hx.shape
    Lc = hc.shape[1]
    h = jnp.concatenate([hc, hx], axis=1)
    T = Lc + L
    p = h @ w_in
    q, k, v, og, a_f, a_b = jnp.split(
        p, [GLA_DK, 2 * GLA_DK, 2 * GLA_DK + GLA_DV, 2 * GLA_DK + 2 * GLA_DV,
            2 * GLA_DK + 2 * GLA_DV + GLA_RANK], axis=-1)

    def heads(a):
        return a.reshape(B, T, GLA_HEADS, -1).transpose(0, 2, 1, 3)

    q = heads(q) * (GLA_DK_HEAD ** -0.5)
    k, v = heads(k), heads(v)

    def log_decay(a, d):
        return heads(jax.nn.log_sigmoid((a @ wa2[d] + ba[d]).astype(F32)) / GLA_TAU)

    g_f, g_b = log_decay(a_f, 0), log_decay(a_b, 1)
    s0 = jnp.zeros((B, GLA_HEADS, GLA_DK_HEAD, GLA_DV_HEAD), F32)
    o_f, _ = gla_chunk_scan(q, k, v, g_f, s0)

    def rev(a):
        return jnp.concatenate([jnp.flip(a[:, :, :Lc], 2), jnp.flip(a[:, :, Lc:], 2)], axis=2)

    o_b, _ = gla_chunk_scan(rev(q), rev(k), rev(v), rev(g_b), s0)
    o = o_f + rev(o_b)
    o = rmsnorm(o, onorm_g.reshape(GLA_HEADS, 1, GLA_DV_HEAD))
    o = o.transpose(0, 2, 1, 3).reshape(B, T, GLA_DV).astype(hx.dtype) * jax.nn.silu(og)
    yx = o[:, Lc:] @ wo
    yc = (o[:, :Lc] @ wo) if need_ctx else None
    return yx, yc


def sink_softmax(logits, sink):
    s = jnp.broadcast_to(sink.astype(F32).reshape(SWA_KV_HEADS, SWA_GROUP, 1, 1),
                         logits.shape[:-1] + (1,))
    return jax.nn.softmax(jnp.concatenate([s, logits], axis=-1), axis=-1)[..., 1:]


def swa_mixer(hx, hc, w_in, sink, wo, cos, sin, need_ctx):
    B, L, _ = hx.shape
    Lc = hc.shape[1]
    scale = SWA_HEAD_DIM ** -0.5
    nq, nk = SWA_HEADS * SWA_HEAD_DIM, SWA_KV_HEADS * SWA_HEAD_DIM

    def proj(h):
        p = h @ w_in
        n = h.shape[1]
        q = p[..., :nq].reshape(B, n, SWA_KV_HEADS, SWA_GROUP, SWA_HEAD_DIM)
        k = p[..., nq:nq + nk].reshape(B, n, SWA_KV_HEADS, SWA_HEAD_DIM)
        v = p[..., nq + nk:].reshape(B, n, SWA_KV_HEADS, SWA_HEAD_DIM)
        return q, k, v

    qc, kc, vc = proj(hc)
    qx, kx, vx = proj(hx)
    qx = apply_axial_rope(qx, cos[:, None, None, :], sin[:, None, None, :]) * scale
    kx = apply_axial_rope(kx, cos[:, None, :], sin[:, None, :])
    qc = qc * scale

    pad = ((0, 0), (SWA_WINDOW, SWA_WINDOW), (0, 0), (0, 0))
    kx_p, vx_p = jnp.pad(kx, pad), jnp.pad(vx, pad)
    span = SWA_BLOCK + 2 * SWA_WINDOW

    def block(j):
        start = j * SWA_BLOCK
        qb = lax.dynamic_slice_in_dim(qx, start, SWA_BLOCK, axis=1)
        kb = lax.dynamic_slice_in_dim(kx_p, start, span, axis=1)
        vb = lax.dynamic_slice_in_dim(vx_p, start, span, axis=1)
        qpos = start + jnp.arange(SWA_BLOCK)
        kpos = start - SWA_WINDOW + jnp.arange(span)
        valid = ((jnp.abs(qpos[:, None] - kpos[None, :]) <= SWA_WINDOW)
                 & (kpos[None, :] >= 0) & (kpos[None, :] < L))
        lw = jnp.where(valid, jnp.einsum('bqkgd,bskd->bkgqs', qb, kb).astype(F32), NEG_INF)
        lc = jnp.einsum('bqkgd,bskd->bkgqs', qb, kc).astype(F32)
        p = sink_softmax(jnp.concatenate([lc, lw], axis=-1), sink).astype(vb.dtype)
        return (jnp.einsum('bkgqs,bskd->bqkgd', p[..., :Lc], vc)
                + jnp.einsum('bkgqs,bskd->bqkgd', p[..., Lc:], vb))

    o = lax.map(block, jnp.arange(L // SWA_BLOCK))
    o = jnp.moveaxis(o, 0, 1).reshape(B, L, nq)
    yx = o @ wo
    yc = None
    if need_ctx:
        pc = sink_softmax(jnp.einsum('bqkgd,bskd->bkgqs', qc, kc).astype(F32), sink).astype(vc.dtype)
        oc = jnp.einsum('bkgqs,bskd->bqkgd', pc, vc).reshape(B, Lc, nq)
        yc = oc @ wo
    return yx, yc


def gmlp_mixer(h, w_in, ln_g, ln_b, ws, bs, wo):
    B, T, _ = h.shape
    p = jax.nn.gelu(h @ w_in)
    u, v = p[..., :GMLP_WIDTH], p[..., GMLP_WIDTH:]
    v = layernorm(v, ln_g, ln_b)
    vg = v.reshape(B, T // GMLP_CHUNK, GMLP_CHUNK, GMLP_GROUPS, GMLP_GROUP_DIM)
    mixed = jnp.einsum('gij,bnjgc->bnigc', ws, vg) + bs.T[None, None, :, :, None]
    return (u * mixed.reshape(B, T, GMLP_WIDTH)) @ wo


def setup_inputs(seed: int = 0) -> dict:
    key = jax.random.key(seed)
    ks = jax.random.split(key, 26)
    D, F = D_MODEL, D_FF
    nrm = jax.random.normal
    return {
        "x": nrm(ks[0], (BATCH, SEQ, D), F32),
        "c": nrm(ks[1], (BATCH, D), F32),
        "ctx": nrm(ks[2], (BATCH, CTX_LEN, D), F32),
        "c_ctx": nrm(ks[3], (D,), F32),
        "ada_w": nrm(ks[4], (DEPTH, D, 6 * D), F32) * (0.5 * D ** -0.5),
        "ada_b": nrm(ks[5], (DEPTH, 6 * D), F32) * 0.02,
        "norm_g": 1.0 + 0.05 * nrm(ks[6], (DEPTH, 4, D), F32),
        "ffn_w1": nrm(ks[7], (DEPTH, D, F), F32) * D ** -0.5,
        "ffn_w3": nrm(ks[8], (DEPTH, D, F), F32) * D ** -0.5,
        "ffn_w2": nrm(ks[9], (DEPTH, F, D), F32) * F ** -0.5,
        "gla_w_in": nrm(ks[10], (N_A, D, GLA_IN), F32) * D ** -0.5,
        "gla_wa2": nrm(ks[11], (N_A, 2, GLA_RANK, GLA_DK), F32) * GLA_RANK ** -0.5,
        "gla_ba": nrm(ks[12], (N_A, 2, GLA_DK), F32) * 0.1,
        "gla_onorm_g": 1.0 + 0.05 * nrm(ks[13], (N_A, GLA_DV), F32),
        "gla_wo": nrm(ks[14], (N_A, GLA_DV, D), F32) * GLA_DV ** -0.5,
        "attn_w_in": nrm(ks[15], (N_B, D, SWA_IN), F32) * D ** -0.5,
        "attn_sink": nrm(ks[16], (N_B, SWA_HEADS), F32) * 0.5,
        "attn_wo": nrm(ks[17], (N_B, SWA_HEADS * SWA_HEAD_DIM, D), F32) * (SWA_HEADS * SWA_HEAD_DIM) ** -0.5,
        "gmlp_w_in": nrm(ks[18], (N_C, D, 2 * GMLP_WIDTH), F32) * D ** -0.5,
        "gmlp_ln_g": 1.0 + 0.05 * nrm(ks[19], (N_C, GMLP_WIDTH), F32),
        "gmlp_ln_b": 0.02 * nrm(ks[20], (N_C, GMLP_WIDTH), F32),
        "gmlp_ws": nrm(ks[21], (N_C, GMLP_GROUPS, GMLP_CHUNK, GMLP_CHUNK), F32) * GMLP_CHUNK ** -0.5,
        "gmlp_bs": 1.0 + 0.1 * nrm(ks[22], (N_C, GMLP_GROUPS, GMLP_CHUNK), F32),
        "gmlp_wo": nrm(ks[23], (N_C, GMLP_WIDTH, D), F32) * GMLP_WIDTH ** -0.5,
    }


def reference(x, c, ctx, c_ctx, ada_w, ada_b, norm_g, ffn_w1, ffn_w3, ffn_w2,
              gla_w_in, gla_wa2, gla_ba, gla_onorm_g, gla_wo,
              attn_w_in, attn_sink, attn_wo,
              gmlp_w_in, gmlp_ln_g, gmlp_ln_b, gmlp_ws, gmlp_bs, gmlp_wo):
    L = x.shape[1]
    cos, sin = axial_rope_tables(L)
    cx = ctx
    for i in range(DEPTH):
        last = i == DEPTH - 1
        kind, slot = i % N_MIXERS, i // N_MIXERS
        mod_x = (jax.nn.silu(c) @ ada_w[i] + ada_b[i])[:, None, :]
        mod_c = (jax.nn.silu(c_ctx) @ ada_w[i] + ada_b[i])[None, None, :]
        shm, scm, gm, shf, scf, gf = jnp.split(mod_x, 6, axis=-1)
        shm_c, scm_c, gm_c, shf_c, scf_c, gf_c = jnp.split(mod_c, 6, axis=-1)

        hx = rmsnorm(x, norm_g[i, 0]) * (1.0 + scm) + shm
        hc = rmsnorm(cx, norm_g[i, 0]) * (1.0 + scm_c) + shm_c
        if kind == 0:
            yx, yc = gla_mixer(hx, hc, gla_w_in[slot], gla_wa2[slot], gla_ba[slot],
                               gla_onorm_g[slot], gla_wo[slot], not last)
        elif kind == 1:
            yx, yc = swa_mixer(hx, hc, attn_w_in[slot], attn_sink[slot], attn_wo[slot],
                               cos, sin, not last)
        else:
            gp = (gmlp_w_in[slot], gmlp_ln_g[slot], gmlp_ln_b[slot], gmlp_ws[slot],
                  gmlp_bs[slot], gmlp_wo[slot])
            yx = gmlp_mixer(hx, *gp)
            yc = None if last else gmlp_mixer(hc, *gp)
        x = x + gm * rmsnorm(yx, norm_g[i, 1])
        if not last:
            cx = cx + gm_c * rmsnorm(yc, norm_g[i, 1])

        hx = rmsnorm(x, norm_g[i, 2]) * (1.0 + scf) + shf
        x = x + gf * rmsnorm(swiglu(hx, ffn_w1[i], ffn_w3[i], ffn_w2[i]), norm_g[i, 3])
        if not last:
            hc = rmsnorm(cx, norm_g[i, 2]) * (1.0 + scf_c) + shf_c
            cx = cx + gf_c * rmsnorm(swiglu(hc, ffn_w1[i], ffn_w3[i], ffn_w2[i]), norm_g[i, 3])
    return x
```

```python
import functools

import numpy as np
import jax
import jax.numpy as jnp
from jax import lax
from jax.experimental import pallas as pl
from jax.experimental.pallas import tpu as pltpu

F32 = jnp.float32
BF16 = jnp.bfloat16

D = 2048
B = 4
L = 2048
LC = 256
DEPTH = 4
FF = 5632
EPS = 1e-6
NEG_INF = -1e30

RC = B * LC
RX = B * L
R = RC + RX

GH = 4
GDK = 256
GDV = 512
GRANK = 16
GCH = 64
GN_CH = (LC + L) // GCH
GC_CH = LC // GCH
G_LEVELS = 6

HD = 64
NH = 32
NKV = 4
GRP = 8
WIN = 128
QB = 128
SWA_N = (NH + 2 * NKV) * HD

MC = 128
MG = 16

VMEM_LIMIT = 56 * 1024 * 1024


def _cparams(sem):
    return pltpu.CompilerParams(dimension_semantics=sem, vmem_limit_bytes=VMEM_LIMIT)


def _mod_row(i, tm):
    nct = RC // tm
    return jnp.where(i < nct, 0, 1 + (i - nct) // (L // tm))


def _norm_mod(x, ng, sc, sh):
    ms = jnp.mean(x * x, axis=-1, keepdims=True)
    return (x * lax.rsqrt(ms + EPS) * ng) * (1.0 + sc) + sh


def _rms(y, g):
    ms = jnp.mean(y * y, axis=-1, keepdims=True)
    return y * lax.rsqrt(ms + EPS) * g


def _silu(a):
    return a * jax.nn.sigmoid(a)


def _dot(a, b):
    return jnp.dot(a, b, preferred_element_type=F32)


def _dot_nt(a, b):
    return lax.dot_general(a, b, (((1,), (1,)), ((), ())), preferred_element_type=F32)


def _dot_tn(a, b):
    return lax.dot_general(a, b, (((0,), (0,)), ((), ())), preferred_element_type=F32)


def _mod_body(c_ref, w_ref, b_ref, o_ref):
    s = _silu(c_ref[...]).astype(BF16)
    o_ref[0] = _dot(s, w_ref[0].astype(BF16)) + b_ref[0]


def _modulation(cvec, ada_w, ada_b):
    tn = 1024
    n6 = 6 * D
    return pl.pallas_call(
        _mod_body,
        grid=(DEPTH, n6 // tn),
        in_specs=[
            pl.BlockSpec((8, D), lambda l, n: (0, 0)),
            pl.BlockSpec((1, D, tn), lambda l, n: (l, 0, n)),
            pl.BlockSpec((1, 1, tn), lambda l, n: (l, 0, n)),
        ],
        out_specs=pl.BlockSpec((1, 8, tn), lambda l, n: (l, 0, n)),
        out_shape=jax.ShapeDtypeStruct((DEPTH, 8, n6), F32),
        compiler_params=_cparams(("parallel", "parallel")),
        name="adaln_mod",
    )(cvec, ada_w, ada_b.reshape(DEPTH, 1, n6))


def _proj_body(kind, has_small, tm, tn, x_ref, mod_ref, ng_ref, w_ref, *rest):
    rest = list(rest)
    if kind == "rope":
        cos_ref, sin_ref = rest.pop(0), rest.pop(0)
    if has_small:
        ws_ref = rest.pop(0)
    o_ref = rest.pop(0)
    if has_small:
        os_ref = rest.pop(0)
    h_scr = rest.pop(0)
    i = pl.program_id(0)
    n = pl.program_id(1)

    @pl.when(n == 0)
    def _():
        h = _norm_mod(x_ref[...], ng_ref[0:1, :], mod_ref[0, 1:2, :], mod_ref[0, 0:1, :])
        hb = h.astype(BF16)
        h_scr[...] = hb
        if has_small:
            os_ref[...] = _dot(hb, ws_ref[...]).astype(os_ref.dtype)

    acc = _dot(h_scr[...], w_ref[...])
    if kind == "none":
        o_ref[...] = acc.astype(o_ref.dtype)
    elif kind == "gelu":
        o_ref[...] = jax.nn.gelu(acc, approximate=True).astype(o_ref.dtype)
    else:
        scale = jnp.where(n < 8, HD ** -0.5, 1.0).astype(F32)
        do_rope = jnp.logical_and(i >= RC // tm, n < 9)

        @pl.when(do_rope)
        def _():
            lane = lax.broadcasted_iota(jnp.int32, acc.shape, 1)
            first = (lane % 32) < 16
            rot = jnp.where(first, -pltpu.roll(acc, tn - 16, 1), pltpu.roll(acc, 16, 1))
            o_ref[...] = ((acc * cos_ref[...] + rot * sin_ref[...]) * scale).astype(o_ref.dtype)

        @pl.when(jnp.logical_not(do_rope))
        def _():
            o_ref[...] = (acc * scale).astype(o_ref.dtype)


def _proj(S, mod, ng, w, *, tm, tn, kind="none", w_small=None, cos=None, sin=None, name="proj"):
    N = w.shape[1]
    nct = RC // tm
    has_small = w_small is not None
    in_specs = [
        pl.BlockSpec((tm, D), lambda i, n: (i, 0)),
        pl.BlockSpec((1, 6, D), lambda i, n: (_mod_row(i, tm), 0, 0)),
        pl.BlockSpec((4, D), lambda i, n: (0, 0)),
        pl.BlockSpec((D, tn), lambda i, n: (0, n)),
    ]
    args = [S, mod, ng, w]
    if kind == "rope":
        pos = lambda i, n: (jnp.where(i < nct, 0, (i - nct) % (L // tm)), 0)
        in_specs += [pl.BlockSpec((tm, tn), pos), pl.BlockSpec((tm, tn), pos)]
        args += [cos, sin]
    out_specs = [pl.BlockSpec((tm, tn), lambda i, n: (i, n))]
    out_shape = [jax.ShapeDtypeStruct((R, N), BF16)]
    if has_small:
        ns = w_small.shape[1]
        in_specs.append(pl.BlockSpec((D, ns), lambda i, n: (0, 0)))
        args.append(w_small)
        out_specs.append(pl.BlockSpec((tm, ns), lambda i, n: (i, 0)))
        out_shape.append(jax.ShapeDtypeStruct((R, ns), BF16))
    res = pl.pallas_call(
        functools.partial(_proj_body, kind, has_small, tm, tn),
        grid=(R // tm, N // tn),
        in_specs=in_specs,
        out_specs=out_specs,
        out_shape=out_shape,
        scratch_shapes=[pltpu.VMEM((tm, D), BF16)],
        compiler_params=_cparams(("parallel", "arbitrary")),
        name=name,
    )(*args)
    return res if has_small else res[0]


def _gla_constants():
    c = GCH
    r = np.arange(c)[:, None]
    t = np.arange(c)[None, :]
    blocks = [(t <= r), (t > r)]
    masks = []
    for lv in range(G_LEVELS):
        s = c >> (lv + 1)
        m = r // s
        odd = (m % 2) == 1
        a_odd = (t > s * m) & (t <= r)
        a_even = (t > r) & (t <= s * (m + 1))
        blocks.append(np.where(odd, a_odd, a_even))
        masks.append(((r // s) % 2 == 1) & ((t // s) == (r // s) - 1))
    masks.append(r == t)
    a_f = np.concatenate(blocks, axis=0).astype(np.float32)
    m_f = np.stack(masks).astype(np.float32)
    a_b = np.concatenate([blk[::-1, ::-1] for blk in blocks], axis=0).astype(np.float32)
    m_b = m_f[:, ::-1, ::-1]
    return (jnp.asarray(np.stack([a_f, a_b]), BF16), jnp.asarray(np.stack([m_f, m_b]), F32))


def _gla_body(qf, kf, vf, af, qb, kb, vb, ab, wa_ref, ba_ref, a_ref, m_ref, of_ref, ob_ref, s_scr):
    c = pl.program_id(1)

    @pl.when(c == 0)
    def _():
        s_scr[...] = jnp.zeros_like(s_scr)

    dirs = ((qf, kf, vf, af, of_ref, GCH - 1), (qb, kb, vb, ab, ob_ref, 0))
    for d, (q_ref, k_ref, v_ref, ar_ref, o_ref, last) in enumerate(dirs):
        araw = _dot(ar_ref[...], wa_ref[d]) + ba_ref[d:d + 1, :]
        g = (jnp.minimum(araw, 0.0) - jnp.log1p(jnp.exp(-jnp.abs(araw)))) * (1.0 / 16.0)
        g1 = g.astype(BF16)
        r1 = g - g1.astype(F32)
        g2 = r1.astype(BF16)
        g3 = (r1 - g2.astype(F32)).astype(BF16)
        amat = a_ref[d]
        e_all = jnp.exp(_dot(amat, g1) + _dot(amat, g2) + _dot(amat, g3))
        for h in range(GH):
            sl = slice(h * GDK, (h + 1) * GDK)
            vs = slice(h * GDV, (h + 1) * GDV)
            q = q_ref[:, sl].astype(F32) * (GDK ** -0.5)
            k = k_ref[:, sl].astype(F32)
            v = v_ref[:, vs]
            e_cum = e_all[0:GCH, sl]
            e_rem = e_all[GCH:2 * GCH, sl]
            st = s_scr[d, h]
            o = _dot_nt((q * e_cum).astype(BF16), st.astype(BF16))
            att = m_ref[d, G_LEVELS] * _dot_nt(q.astype(BF16), k.astype(BF16))
            for lv in range(G_LEVELS):
                e_l = e_all[(2 + lv) * GCH:(3 + lv) * GCH, sl]
                att = att + m_ref[d, lv] * _dot_nt((q * e_l).astype(BF16), (k * e_l).astype(BF16))
            o = o + _dot(att.astype(BF16), v)
            o_ref[:, vs] = o
            s_scr[d, h] = st * e_cum[last:last + 1, :] + _dot_tn(v, (k * e_rem).astype(BF16))


def _gla_scan(p, a, wa2p, ba, amat, masks):
    def rowblk(b, m):
        return jnp.where(m < GC_CH, b * GC_CH + m, RC // GCH + b * (L // GCH) + (m - GC_CH))

    def bwd(c):
        return jnp.where(c < GC_CH, GC_CH - 1 - c, GN_CH + GC_CH - 1 - c)

    hk = GH * GDK
    hv = GH * GDV

    def specs(order):
        return [
            pl.BlockSpec((GCH, hk), lambda b, c: (rowblk(b, order(c)), 0)),
            pl.BlockSpec((GCH, hk), lambda b, c: (rowblk(b, order(c)), 1)),
            pl.BlockSpec((GCH, hv), lambda b, c: (rowblk(b, order(c)), 1)),
        ]

    fwd = lambda c: c
    in_specs = (
        specs(fwd) + [pl.BlockSpec((GCH, 128), lambda b, c: (rowblk(b, c), 0))]
        + specs(bwd) + [pl.BlockSpec((GCH, 128), lambda b, c: (rowblk(b, bwd(c)), 0))]
        + [
            pl.BlockSpec((2, 128, hk), lambda b, c: (0, 0, 0)),
            pl.BlockSpec((2, hk), lambda b, c: (0, 0)),
            pl.BlockSpec((2, 8 * GCH, GCH), lambda b, c: (0, 0, 0)),
            pl.BlockSpec((2, G_LEVELS + 1, GCH, GCH), lambda b, c: (0, 0, 0, 0)),
        ]
    )
    out_specs = [
        pl.BlockSpec((GCH, hv), lambda b, c: (rowblk(b, c), 0)),
        pl.BlockSpec((GCH, hv), lambda b, c: (rowblk(b, bwd(c)), 0)),
    ]
    return pl.pallas_call(
        _gla_body,
        grid=(B, GN_CH),
        in_specs=in_specs,
        out_specs=out_specs,
        out_shape=[jax.ShapeDtypeStruct((R, hv), F32)] * 2,
        scratch_shapes=[pltpu.VMEM((2, GH, GDV, GDK), F32)],
        compiler_params=_cparams(("parallel", "arbitrary")),
        name="gla_scan",
    )(p, p, p, a, p, p, p, a, wa2p, ba, amat, masks)


def _swa_body(sink_ref, q_ref, kc_ref, vc_ref, k0, k1, k2, v0, v1, v2, o_ref):
    j = pl.program_id(1)
    is_ctx = j < LC // QB
    jl = j - LC // QB
    lo = jnp.where(is_ctx, 0, jnp.where(jl >= 1, 0, QB))
    hi = jnp.where(is_ctx, 0, jnp.where(jl <= L // QB - 2, 3 * QB, 2 * QB))
    rr = lax.broadcasted_iota(jnp.int32, (GRP * QB, 3 * QB), 0) % QB
    tt = lax.broadcasted_iota(jnp.int32, (GRP * QB, 3 * QB), 1)
    dlt = tt - rr
    wmask = (dlt >= 0) & (dlt <= 2 * WIN) & (tt >= lo) & (tt < hi)

    kc = kc_ref[...].astype(F32)
    vc = vc_ref[...].astype(F32)
    kw = jnp.concatenate([k0[...], k1[...], k2[...]], axis=0).astype(F32)
    vw = jnp.concatenate([v0[...], v1[...], v2[...]], axis=0).astype(F32)
    lane_q = lax.broadcasted_iota(jnp.int32, (QB, 128), 1)

    def dup(xp, half):
        low = lax.broadcasted_iota(jnp.int32, xp.shape, 1) < HD
        rolled = pltpu.roll(xp, HD, 1)
        out = jnp.where(low, xp, rolled) if half == 0 else jnp.where(low, rolled, xp)
        return out.astype(BF16)

    for kh in range(NKV):
        ps = slice((kh // 2) * 128, (kh // 2 + 1) * 128)
        kcd, kwd = dup(kc[:, ps], kh % 2), dup(kw[:, ps], kh % 2)
        vcd, vwd = dup(vc[:, ps], kh % 2), dup(vw[:, ps], kh % 2)
        qs = []
        for g in range(GRP):
            h = kh * GRP + g
            blk = q_ref[:, (h // 2) * 128:(h // 2 + 1) * 128].astype(F32)
            keep = (lane_q < HD) if h % 2 == 0 else (lane_q >= HD)
            qs.append(jnp.where(keep, blk, 0.0).astype(BF16))
        qst = jnp.concatenate(qs, axis=0)
        lc = _dot_nt(qst, kcd)
        lw = jnp.where(wmask, _dot_nt(qst, kwd), NEG_INF)
        ecs, ews, invs = [], [], []
        for g in range(GRP):
            rs = slice(g * QB, (g + 1) * QB)
            s = sink_ref[kh * GRP + g]
            m = jnp.maximum(jnp.maximum(jnp.max(lc[rs], axis=-1, keepdims=True),
                                        jnp.max(lw[rs], axis=-1, keepdims=True)), s)
            ec = jnp.exp(lc[rs] - m)
            ew = jnp.exp(lw[rs] - m)
            den = (jnp.sum(ec, axis=-1, keepdims=True) + jnp.sum(ew, axis=-1, keepdims=True)
                   + jnp.exp(s - m))
            ecs.append(ec.astype(BF16))
            ews.append(ew.astype(BF16))
            invs.append(1.0 / den)
        res = _dot(jnp.concatenate(ecs, axis=0), vcd) + _dot(jnp.concatenate(ews, axis=0), vwd)
        res = res * jnp.concatenate(invs, axis=0)
        for m2 in range(GRP // 2):
            even = res[(2 * m2) * QB:(2 * m2 + 1) * QB]
            odd = res[(2 * m2 + 1) * QB:(2 * m2 + 2) * QB]
            col = (kh * (GRP // 2) + m2) * 128
            o_ref[:, col:col + 128] = jnp.where(lane_q < HD, even, odd).astype(o_ref.dtype)


def _swa_attention(p, sink):
    ncb = LC // QB
    nlb = L // QB
    lat0 = RC // QB
    nq = NH * HD
    kcol = nq // (NKV * HD)

    def qrow(b, j):
        return jnp.where(j < ncb, b * ncb + j, lat0 + b * nlb + (j - ncb))

    def wrow(off):
        def f(b, j):
            jl = jnp.clip(j - ncb + off, 0, nlb - 1)
            return lat0 + b * nlb + jl
        return f

    kvw = NKV * HD
    in_specs = [
        pl.BlockSpec(memory_space=pltpu.SMEM),
        pl.BlockSpec((QB, nq), lambda b, j: (qrow(b, j), 0)),
        pl.BlockSpec((LC, kvw), lambda b, j: (b, kcol)),
        pl.BlockSpec((LC, kvw), lambda b, j: (b, kcol + 1)),
    ]
    for col in (kcol, kcol + 1):
        for off in (-1, 0, 1):
            in_specs.append(pl.BlockSpec((QB, kvw), functools.partial(
                lambda b, j, f, cc: (f(b, j), cc), f=wrow(off), cc=col)))
    return pl.pallas_call(
        _swa_body,
        grid=(B, ncb + nlb),
        in_specs=in_specs,
        out_specs=pl.BlockSpec((QB, nq), lambda b, j: (qrow(b, j), 0)),
        out_shape=jax.ShapeDtypeStruct((R, nq), BF16),
        compiler_params=_cparams(("parallel", "parallel")),
        name="swa_attn",
    )(sink, p, p, p, p, p, p, p, p, p)


def _gla_prologue(of_ref, ob_ref, og_ref, on_ref):
    o = of_ref[...] + ob_ref[...]
    parts = [_rms(o[:, h * GDV:(h + 1) * GDV], on_ref[:, h * GDV:(h + 1) * GDV]) for h in range(GH)]
    return (jnp.concatenate(parts, axis=-1) * _silu(og_ref[...].astype(F32))).astype(BF16)


def _swa_prologue(o_ref):
    return o_ref[...]


def _gmlp_prologue(u_ref, v_ref, lng_ref, lnb_ref, ws_ref, bsb_ref, lhs_scr):
    v = v_ref[...].astype(F32)
    mu = jnp.mean(v, axis=-1, keepdims=True)
    var = jnp.mean(jnp.square(v - mu), axis=-1, keepdims=True)
    vn = ((v - mu) * lax.rsqrt(var + EPS) * lng_ref[...] + lnb_ref[...]).astype(BF16)
    for ch in range(v.shape[0] // MC):
        rs = slice(ch * MC, (ch + 1) * MC)
        for g in range(MG):
            cs = slice(g * 128, (g + 1) * 128)
            mixed = _dot(ws_ref[g], vn[rs, cs]) + bsb_ref[:, cs]
            lhs_scr[rs, cs] = (u_ref[rs, cs].astype(F32) * mixed).astype(BF16)
    return lhs_scr[...]


def _out_body(prologue, n_in, *refs):
    in_refs = refs[:n_in]
    wo_ref, x_ref, mod_ref, ng_ref, o_ref = refs[n_in:n_in + 5]
    lhs = prologue(*in_refs, *refs[n_in + 5:])
    y = _dot(lhs, wo_ref[...])
    o_ref[...] = x_ref[...] + mod_ref[0, 2:3, :] * _rms(y, ng_ref[1:2, :])


def _out_proj(prologue, ins, in_specs, wo, S, mod, ng, *, tm, skip_ctx, scratch=(), name="out_proj"):
    off = RC // tm if skip_ctx else 0
    rows = RX if skip_ctx else R
    specs = [pl.BlockSpec(bs, functools.partial(lambda i, f: f(i + off), f=f)) for bs, f in in_specs]
    specs += [
        pl.BlockSpec((D, D), lambda i: (0, 0)),
        pl.BlockSpec((tm, D), lambda i: (i + off, 0)),
        pl.BlockSpec((1, 6, D), lambda i: (_mod_row(i + off, tm), 0, 0)),
        pl.BlockSpec((4, D), lambda i: (0, 0)),
    ]
    return pl.pallas_call(
        functools.partial(_out_body, prologue, len(ins)),
        grid=(rows // tm,),
        in_specs=specs,
        out_specs=pl.BlockSpec((tm, D), lambda i: (i, 0)),
        out_shape=jax.ShapeDtypeStruct((rows, D), F32),
        scratch_shapes=list(scratch),
        compiler_params=_cparams(("parallel",)),
        name=name,
    )(*ins, wo, S, mod, ng)


def _ffn_body(nf, x_ref, mod_ref, ng_ref, w1_ref, w3_ref, w2_ref, o_ref, h_scr, acc_scr):
    f = pl.program_id(1)

    @pl.when(f == 0)
    def _():
        h = _norm_mod(x_ref[...], ng_ref[2:3, :], mod_ref[0, 4:5, :], mod_ref[0, 3:4, :])
        h_scr[...] = h.astype(BF16)
        acc_scr[...] = jnp.zeros_like(acc_scr)

    hb = h_scr[...]
    a = _dot(hb, w1_ref[...])
    b = _dot(hb, w3_ref[...])
    acc_scr[...] += _dot((_silu(a) * b).astype(BF16), w2_ref[...])

    @pl.when(f == nf - 1)
    def _():
        o_ref[...] = x_ref[...] + mod_ref[0, 5:6, :] * _rms(acc_scr[...], ng_ref[3:4, :])


def _ffn(S, mod, ng, w1, w3, w2, *, tm, tf, rows_total):
    nf = FF // tf
    return pl.pallas_call(
        functools.partial(_ffn_body, nf),
        grid=(S.shape[0] // tm, nf),
        in_specs=[
            pl.BlockSpec((tm, D), lambda i, f: (i, 0)),
            pl.BlockSpec((1, 6, D), lambda i, f: (_mod_row(i + (rows_total - S.shape[0]) // tm, tm), 0, 0)),
            pl.BlockSpec((4, D), lambda i, f: (0, 0)),
            pl.BlockSpec((D, tf), lambda i, f: (0, f)),
            pl.BlockSpec((D, tf), lambda i, f: (0, f)),
            pl.BlockSpec((tf, D), lambda i, f: (f, 0)),
        ],
        out_specs=pl.BlockSpec((tm, D), lambda i, f: (i, 0)),
        out_shape=jax.ShapeDtypeStruct(S.shape, F32),
        scratch_shapes=[pltpu.VMEM((tm, D), BF16), pltpu.VMEM((tm, D), F32)],
        compiler_params=_cparams(("parallel", "arbitrary")),
        name="ffn",
    )(S, mod, ng, w1, w3, w2)


def _rope_tables():
    quarter = HD // 4
    inv_freq = 10000.0 ** (-jnp.arange(quarter, dtype=F32) / quarter)
    row = jnp.repeat(jnp.arange(L // 64), 64).astype(F32)
    col = jnp.tile(jnp.arange(64), L // 64).astype(F32)
    ang_r = row[:, None] * inv_freq
    ang_c = col[:, None] * inv_freq
    ang = jnp.concatenate([ang_r, ang_r, ang_c, ang_c], axis=-1)
    return jnp.tile(jnp.cos(ang), (1, 4)), jnp.tile(jnp.sin(ang), (1, 4))


def kernel(x, c, ctx, c_ctx, ada_w, ada_b, norm_g, ffn_w1, ffn_w3, ffn_w2,
           gla_w_in, gla_wa2, gla_ba, gla_onorm_g, gla_wo,
           attn_w_in, attn_sink, attn_wo,
           gmlp_w_in, gmlp_ln_g, gmlp_ln_b, gmlp_ws, gmlp_bs, gmlp_wo):
    TM = 512
    S = jnp.concatenate([ctx.reshape(RC, D), x.reshape(RX, D)], axis=0)
    cvec = jnp.concatenate([c_ctx[None, :], c, jnp.zeros((3, D), F32)], axis=0)
    mods = _modulation(cvec, ada_w, ada_b).reshape(DEPTH, 8, 6, D)
    rowtile = lambda i: (i, 0)

    for i in range(DEPTH):
        last = i == DEPTH - 1
        kind, slot = i % 3, i // 3
        mod, ng = mods[i], norm_g[i]
        if kind == 0:
            w_in = gla_w_in[slot]
            nmain = 2 * GH * GDK + 2 * GH * GDV
            w_main = w_in[:, :nmain].astype(BF16)
            w_a = jnp.pad(w_in[:, nmain:], ((0, 0), (0, 128 - 2 * GRANK))).astype(BF16)
            p, a = _proj(S, mod, ng, w_main, tm=TM, tn=512, w_small=w_a, name="gla_proj")
            wa2 = gla_wa2[slot]
            wa2p = jnp.zeros((2, 128, GH * GDK), F32)
            wa2p = wa2p.at[0, :GRANK].set(wa2[0]).at[1, GRANK:2 * GRANK].set(wa2[1]).astype(BF16)
            amat, masks = _gla_constants()
            o_f, o_b = _gla_scan(p, a, wa2p, gla_ba[slot], amat, masks)
            hv = GH * GDV
            ins = [o_f, o_b, p, gla_onorm_g[slot].reshape(1, hv)]
            in_specs = [((TM, hv), rowtile), ((TM, hv), rowtile),
                        ((TM, hv), lambda i: (i, 2)), ((1, hv), lambda i: (0, 0))]
            S = _out_proj(_gla_prologue, ins, in_specs, gla_wo[slot].astype(BF16), S, mod, ng,
                          tm=TM, skip_ctx=last, name="gla_out")
        elif kind == 1:
            cos, sin = _rope_tables()
            p = _proj(S, mod, ng, attn_w_in[slot].astype(BF16), tm=TM, tn=256, kind="rope",
                      cos=cos, sin=sin, name="swa_proj")
            o = _swa_attention(p, attn_sink[slot])
            S = _out_proj(_swa_prologue, [o], [((TM, NH * HD), rowtile)], attn_wo[slot].astype(BF16),
                          S, mod, ng, tm=TM, skip_ctx=last, name="swa_out")
        else:
            p = _proj(S, mod, ng, gmlp_w_in[slot].astype(BF16), tm=TM, tn=512, kind="gelu",
                      name="gmlp_proj")
            bsb = jnp.repeat(gmlp_bs[slot].T, 128, axis=1)
            ins = [p, p, gmlp_ln_g[slot].reshape(1, D), gmlp_ln_b[slot].reshape(1, D),
                   gmlp_ws[slot].astype(BF16), bsb]
            in_specs = [((TM, D), rowtile), ((TM, D), lambda i: (i, 1)),
                        ((1, D), lambda i: (0, 0)), ((1, D), lambda i: (0, 0)),
                        ((MG, MC, MC), lambda i: (0, 0, 0)), ((MC, D), lambda i: (0, 0))]
            S = _out_proj(_gmlp_prologue, ins, in_specs, gmlp_wo[slot].astype(BF16), S, mod, ng,
                          tm=TM, skip_ctx=last, scratch=[pltpu.VMEM((TM, D), BF16)], name="gmlp_out")
        rows_total = R
        S = _ffn(S, mod, ng, ffn_w1[i].astype(BF16), ffn_w3[i].astype(BF16), ffn_w2[i].astype(BF16),
                 tm=TM, tf=512, rows_total=rows_total)
    return S.reshape(B, L, D)
```

```python
import functools

import numpy as np
import jax
import jax.numpy as jnp
from jax import lax
from jax.experimental import pallas as pl
from jax.experimental.pallas import tpu as pltpu

F32 = jnp.float32
BF16 = jnp.bfloat16

D = 2048
B = 4
L = 2048
LC = 256
DEPTH = 4
FF = 5632
EPS = 1e-6
NEG_INF = -1e30

RC = B * LC
RX = B * L
R = RC + RX

GH = 4
GDK = 256
GDV = 512
GRANK = 16
GCH = 128
GN_CH = (LC + L) // GCH
GC_CH = LC // GCH
G_LEVELS = 7
G_ROWS = (2 + G_LEVELS) * GCH

HD = 64
NH = 32
NKV = 4
GRP = 8
WIN = 128
QB = 128
NQ = NH * HD
NKVD = NKV * HD

MC = 128
MG = 16

TM_PROJ = 1024
TN_PROJ = 512
TM_OUT = 512
TM_FFN = 1024
TF_FFN = 512

VMEM_LIMIT = 56 * 1024 * 1024


def _cparams(sem):
    return pltpu.CompilerParams(dimension_semantics=sem, vmem_limit_bytes=VMEM_LIMIT)


def _mod_row(i, tm):
    nct = RC // tm
    return jnp.where(i < nct, 0, 1 + (i - nct) // (L // tm))


def _norm_mod(x, ng, sc, sh):
    ms = jnp.mean(x * x, axis=-1, keepdims=True)
    return (x * lax.rsqrt(ms + EPS) * ng) * (1.0 + sc) + sh


def _rms(y, g):
    ms = jnp.mean(y * y, axis=-1, keepdims=True)
    return y * lax.rsqrt(ms + EPS) * g


def _silu(a):
    return a * jax.nn.sigmoid(a)


def _dot(a, b):
    return jnp.dot(a, b, preferred_element_type=F32)


def _dot_nt(a, b):
    return lax.dot_general(a, b, (((1,), (1,)), ((), ())), preferred_element_type=F32)


def _dot_tn(a, b):
    return lax.dot_general(a, b, (((0,), (0,)), ((), ())), preferred_element_type=F32)


def _mod_body(c_ref, w_ref, b_ref, o_ref):
    s = _silu(c_ref[...]).astype(BF16)
    o_ref[0] = _dot(s, w_ref[0].astype(BF16)) + b_ref[0]


def _modulation(cvec, ada_w, ada_b):
    tn = 1024
    n6 = 6 * D
    return pl.pallas_call(
        _mod_body,
        grid=(DEPTH, n6 // tn),
        in_specs=[
            pl.BlockSpec((8, D), lambda l, n: (0, 0)),
            pl.BlockSpec((1, D, tn), lambda l, n: (l, 0, n)),
            pl.BlockSpec((1, 1, tn), lambda l, n: (l, 0, n)),
        ],
        out_specs=pl.BlockSpec((1, 8, tn), lambda l, n: (l, 0, n)),
        out_shape=jax.ShapeDtypeStruct((DEPTH, 8, n6), F32),
        compiler_params=_cparams(("parallel", "parallel")),
        name="adaln_mod",
    )(cvec, ada_w, ada_b.reshape(DEPTH, 1, n6))


def _proj_body(kind, has_small, tm, tn, x_ref, mod_ref, ng_ref, w_ref, *rest):
    rest = list(rest)
    if kind == "rope":
        cos_ref, sin_ref = rest.pop(0), rest.pop(0)
    if has_small:
        ws_ref = rest.pop(0)
    o_ref = rest.pop(0)
    if has_small:
        os_ref = rest.pop(0)
    h_scr = rest.pop(0)
    i = pl.program_id(0)
    n = pl.program_id(1)

    @pl.when(n == 0)
    def _():
        h = _norm_mod(x_ref[...], ng_ref[0, 0:1, :], mod_ref[0, 0, 1:2, :], mod_ref[0, 0, 0:1, :])
        hb = h.astype(BF16)
        h_scr[...] = hb
        if has_small:
            os_ref[...] = _dot(hb, ws_ref[0]).astype(os_ref.dtype)

    acc = _dot(h_scr[...], w_ref[0])
    if kind == "none":
        o_ref[...] = acc.astype(o_ref.dtype)
    elif kind == "gelu":
        o_ref[...] = jax.nn.gelu(acc, approximate=True).astype(o_ref.dtype)
    else:
        lane = lax.broadcasted_iota(jnp.int32, acc.shape, 1)
        first = (lane % 32) < 16
        rot = jnp.where(first, -pltpu.roll(acc, tn - 16, 1), pltpu.roll(acc, 16, 1))
        roped = acc * cos_ref[...] + rot * sin_ref[...]
        rope_cols = jnp.where(i >= RC // tm, NQ + NKVD, 0)
        scale = jnp.where(n < NQ // tn, HD ** -0.5, 1.0).astype(F32)
        o_ref[...] = (jnp.where(n * tn + lane < rope_cols, roped, acc) * scale).astype(o_ref.dtype)


def _proj(S, mods, norm_g, layer, w, slot, n_cols, *, kind="none", w_small=None, cos=None, sin=None,
          name="proj"):
    tm, tn = TM_PROJ, TN_PROJ
    nct = RC // tm
    has_small = w_small is not None
    in_specs = [
        pl.BlockSpec((tm, D), lambda i, n: (i, 0), pipeline_mode=pl.Buffered(1)),
        pl.BlockSpec((1, 1, 6, D), lambda i, n: (layer, _mod_row(i, tm), 0, 0)),
        pl.BlockSpec((1, 4, D), lambda i, n: (layer, 0, 0)),
        pl.BlockSpec((1, D, tn), lambda i, n: (slot, 0, n)),
    ]
    args = [S, mods, norm_g, w]
    if kind == "rope":
        pos = lambda i, n: (jnp.where(i < nct, 0, (i - nct) % (L // tm)), 0)
        in_specs += [pl.BlockSpec((tm, tn), pos), pl.BlockSpec((tm, tn), pos)]
        args += [cos, sin]
    out_specs = [pl.BlockSpec((tm, tn), lambda i, n: (i, n))]
    out_shape = [jax.ShapeDtypeStruct((R, n_cols), BF16)]
    if has_small:
        ns = w_small.shape[2]
        in_specs.append(pl.BlockSpec((1, D, ns), lambda i, n: (slot, 0, 0)))
        args.append(w_small)
        out_specs.append(pl.BlockSpec((tm, ns), lambda i, n: (i, 0)))
        out_shape.append(jax.ShapeDtypeStruct((R, ns), BF16))
    res = pl.pallas_call(
        functools.partial(_proj_body, kind, has_small, tm, tn),
        grid=(R // tm, n_cols // tn),
        in_specs=in_specs,
        out_specs=out_specs,
        out_shape=out_shape,
        scratch_shapes=[pltpu.VMEM((tm, D), BF16)],
        compiler_params=_cparams(("parallel", "arbitrary")),
        name=name,
    )(*args)
    return res if has_small else res[0]


def _gla_constants():
    c = GCH
    r = np.arange(c)[:, None]
    t = np.arange(c)[None, :]
    blocks = [(t <= r), (t > r)]
    masks = []
    for lv in range(G_LEVELS):
        s = c >> (lv + 1)
        m = r // s
        odd = (m % 2) == 1
        a_odd = (t > s * m) & (t <= r)
        a_even = (t > r) & (t <= s * (m + 1))
        blocks.append(np.where(odd, a_odd, a_even))
        masks.append(odd & ((t // s) == m - 1))
    masks.append(r == t)
    a_f = np.concatenate(blocks, axis=0).astype(np.float32)
    m_f = np.stack(masks).astype(np.float32)
    a_b = np.concatenate([blk[::-1, ::-1] for blk in blocks], axis=0).astype(np.float32)
    m_b = m_f[:, ::-1, ::-1]
    amat = np.stack([a_f, a_b])
    amat = np.concatenate([amat, amat], axis=2)
    return jnp.asarray(amat, BF16), jnp.asarray(np.stack([m_f, m_b]), F32)


def _gla_body(qf, kf, vf, af, qb, kb, vb, ab, wa_ref, ba_ref, a_ref, m_ref, of_ref, ob_ref, s_scr):
    c = pl.program_id(1)

    @pl.when(c == 0)
    def _():
        s_scr[...] = jnp.zeros_like(s_scr)

    row = lax.broadcasted_iota(jnp.int32, (GCH, GDK), 0)
    dirs = ((qf, kf, vf, af, of_ref, GCH - 1), (qb, kb, vb, ab, ob_ref, 0))
    for d, (q_ref, k_ref, v_ref, ar_ref, o_ref, last) in enumerate(dirs):
        araw = _dot(ar_ref[...], wa_ref[d]) + ba_ref[d:d + 1, :]
        g = (jnp.minimum(araw, 0.0) - jnp.log1p(jnp.exp(-jnp.abs(araw)))) * (1.0 / 16.0)
        g1 = g.astype(BF16)
        g2 = (g - g1.astype(F32)).astype(BF16)
        e_all = jnp.exp(_dot(a_ref[d], jnp.concatenate([g1, g2], axis=0)))

        def zsel(q, k, lv):
            s = GCH >> (lv + 1)
            if s >= 8:
                parts = [(q if (m % 2 == 1) == (d == 0) else k)[m * s:(m + 1) * s]
                         for m in range(GCH // s)]
                return jnp.concatenate(parts, axis=0)
            odd = ((row // s) % 2) == 1
            return jnp.where(odd, q, k) if d == 0 else jnp.where(odd, k, q)

        for h in range(GH):
            sl = slice(h * GDK, (h + 1) * GDK)
            vs = slice(h * GDV, (h + 1) * GDV)
            q = q_ref[:, sl].astype(F32) * (GDK ** -0.5)
            k = k_ref[:, sl].astype(F32)
            v = v_ref[:, vs]
            e_cum = e_all[0:GCH, sl]
            e_rem = e_all[GCH:2 * GCH, sl]
            st = s_scr[d, h]
            o = _dot_nt((q * e_cum).astype(BF16), st.astype(BF16))
            att = m_ref[d, G_LEVELS] * _dot_nt(q.astype(BF16), k.astype(BF16))
            for lv in range(G_LEVELS):
                z = (zsel(q, k, lv) * e_all[(2 + lv) * GCH:(3 + lv) * GCH, sl]).astype(BF16)
                att = att + m_ref[d, lv] * _dot_nt(z, z)
            o = o + _dot(att.astype(BF16), v)
            o_ref[:, vs] = o
            s_scr[d, h] = st * e_cum[last:last + 1, :] + _dot_tn(v, (k * e_rem).astype(BF16))


def _gla_scan(p, a, wa2p, ba, slot, amat, masks):
    def rowblk(b, m):
        return jnp.where(m < GC_CH, b * GC_CH + m, RC // GCH + b * (L // GCH) + (m - GC_CH))

    def bwd(c):
        return jnp.where(c < GC_CH, GC_CH - 1 - c, GN_CH + GC_CH - 1 - c)

    hk = GH * GDK
    hv = GH * GDV

    def specs(order):
        return [
            pl.BlockSpec((GCH, hk), lambda b, c: (rowblk(b, order(c)), 0)),
            pl.BlockSpec((GCH, hk), lambda b, c: (rowblk(b, order(c)), 1)),
            pl.BlockSpec((GCH, hv), lambda b, c: (rowblk(b, order(c)), 1)),
            pl.BlockSpec((GCH, 128), lambda b, c: (rowblk(b, order(c)), 0)),
        ]

    in_specs = specs(lambda c: c) + specs(bwd) + [
        pl.BlockSpec((2, 128, hk), lambda b, c: (0, 0, 0)),
        pl.BlockSpec((1, 2, hk), lambda b, c: (slot, 0, 0)),
        pl.BlockSpec((2, G_ROWS, 2 * GCH), lambda b, c: (0, 0, 0)),
        pl.BlockSpec((2, G_LEVELS + 1, GCH, GCH), lambda b, c: (0, 0, 0, 0)),
    ]
    out_specs = [
        pl.BlockSpec((GCH, hv), lambda b, c: (rowblk(b, c), 0)),
        pl.BlockSpec((GCH, hv), lambda b, c: (rowblk(b, bwd(c)), 0)),
    ]

    def body(qf, kf, vf, af, qb, kb, vb, ab, wa_ref, ba_ref, a_ref, m_ref, of_ref, ob_ref, s_scr):
        _gla_body(qf, kf, vf, af, qb, kb, vb, ab, wa_ref, ba_ref.at[0], a_ref, m_ref, of_ref, ob_ref, s_scr)

    return pl.pallas_call(
        body,
        grid=(B, GN_CH),
        in_specs=in_specs,
        out_specs=out_specs,
        out_shape=[jax.ShapeDtypeStruct((R, hv), F32)] * 2,
        scratch_shapes=[pltpu.VMEM((2, GH, GDV, GDK), F32)],
        compiler_params=_cparams(("parallel", "arbitrary")),
        name="gla_scan",
    )(p, p, p, a, p, p, p, a, wa2p, ba, amat, masks)


def _swa_body(sink_ref, q_ref, kc_ref, vc_ref, k0, k1, k2, v0, v1, v2, o_ref):
    j = pl.program_id(1)
    is_ctx = j < LC // QB
    jl = j - LC // QB
    lo = jnp.where(is_ctx, 0, jnp.where(jl >= 1, 0, QB))
    hi = jnp.where(is_ctx, 0, jnp.where(jl <= L // QB - 2, 3 * QB, 2 * QB))
    rr = lax.broadcasted_iota(jnp.int32, (GRP * QB, 3 * QB), 0) % QB
    tt = lax.broadcasted_iota(jnp.int32, (GRP * QB, 3 * QB), 1)
    dlt = tt - rr
    wmask = (dlt >= 0) & (dlt <= 2 * WIN) & (tt >= lo) & (tt < hi)

    kc = kc_ref[...].astype(F32)
    vc = vc_ref[...].astype(F32)
    kw = jnp.concatenate([k0[...], k1[...], k2[...]], axis=0).astype(F32)
    vw = jnp.concatenate([v0[...], v1[...], v2[...]], axis=0).astype(F32)
    lane_q = lax.broadcasted_iota(jnp.int32, (QB, 128), 1)

    def dup(xp, half):
        low = lax.broadcasted_iota(jnp.int32, xp.shape, 1) < HD
        rolled = pltpu.roll(xp, HD, 1)
        out = jnp.where(low, xp, rolled) if half == 0 else jnp.where(low, rolled, xp)
        return out.astype(BF16)

    for kh in range(NKV):
        ps = slice((kh // 2) * 128, (kh // 2 + 1) * 128)
        kcd, kwd = dup(kc[:, ps], kh % 2), dup(kw[:, ps], kh % 2)
        vcd, vwd = dup(vc[:, ps], kh % 2), dup(vw[:, ps], kh % 2)
        qs = []
        for g in range(GRP):
            h = kh * GRP + g
            blk = q_ref[:, (h // 2) * 128:(h // 2 + 1) * 128].astype(F32)
            keep = (lane_q < HD) if h % 2 == 0 else (lane_q >= HD)
            qs.append(jnp.where(keep, blk, 0.0).astype(BF16))
        qst = jnp.concatenate(qs, axis=0)
        lc = _dot_nt(qst, kcd)
        lw = jnp.where(wmask, _dot_nt(qst, kwd), NEG_INF)
        ecs, ews, invs = [], [], []
        for g in range(GRP):
            rs = slice(g * QB, (g + 1) * QB)
            s = sink_ref[0, kh * GRP + g]
            m = jnp.maximum(jnp.maximum(jnp.max(lc[rs], axis=-1, keepdims=True),
                                        jnp.max(lw[rs], axis=-1, keepdims=True)), s)
            ec = jnp.exp(lc[rs] - m)
            ew = jnp.exp(lw[rs] - m)
            den = (jnp.sum(ec, axis=-1, keepdims=True) + jnp.sum(ew, axis=-1, keepdims=True)
                   + jnp.exp(s - m))
            ecs.append(ec.astype(BF16))
            ews.append(ew.astype(BF16))
            invs.append(1.0 / den)
        res = _dot(jnp.concatenate(ecs, axis=0), vcd) + _dot(jnp.concatenate(ews, axis=0), vwd)
        res = res * jnp.concatenate(invs, axis=0)
        for m2 in range(GRP // 2):
            even = res[(2 * m2) * QB:(2 * m2 + 1) * QB]
            odd = res[(2 * m2 + 1) * QB:(2 * m2 + 2) * QB]
            col = (kh * (GRP // 2) + m2) * 128
            o_ref[:, col:col + 128] = jnp.where(lane_q < HD, even, odd).astype(o_ref.dtype)


def _swa_attention(p, sink):
    ncb = LC // QB
    nlb = L // QB
    lat0 = RC // QB
    kcol = NQ // NKVD

    def qrow(b, j):
        return jnp.where(j < ncb, b * ncb + j, lat0 + b * nlb + (j - ncb))

    def wrow(off):
        def f(b, j):
            jl = jnp.clip(j - ncb + off, 0, nlb - 1)
            return lat0 + b * nlb + jl
        return f

    in_specs = [
        pl.BlockSpec(memory_space=pltpu.SMEM),
        pl.BlockSpec((QB, NQ), lambda b, j: (qrow(b, j), 0)),
        pl.BlockSpec((LC, NKVD), lambda b, j: (b, kcol)),
        pl.BlockSpec((LC, NKVD), lambda b, j: (b, kcol + 1)),
    ]
    for col in (kcol, kcol + 1):
        for off in (-1, 0, 1):
            in_specs.append(pl.BlockSpec((QB, NKVD), functools.partial(
                lambda b, j, f, cc: (f(b, j), cc), f=wrow(off), cc=col)))
    return pl.pallas_call(
        _swa_body,
        grid=(B, ncb + nlb),
        in_specs=in_specs,
        out_specs=pl.BlockSpec((QB, NQ), lambda b, j: (qrow(b, j), 0)),
        out_shape=jax.ShapeDtypeStruct((R, NQ), BF16),
        compiler_params=_cparams(("parallel", "parallel")),
        name="swa_attn",
    )(sink, p, p, p, p, p, p, p, p, p)


def _gla_prologue(of_ref, ob_ref, og_ref, on_ref):
    o = of_ref[...] + ob_ref[...]
    parts = [_rms(o[:, h * GDV:(h + 1) * GDV], on_ref[0, :, h * GDV:(h + 1) * GDV]) for h in range(GH)]
    return (jnp.concatenate(parts, axis=-1) * _silu(og_ref[...].astype(F32))).astype(BF16)


def _swa_prologue(o_ref):
    return o_ref[...]


def _gmlp_prologue(u_ref, v_ref, lng_ref, lnb_ref, ws_ref, bsb_ref, lhs_scr):
    v = v_ref[...].astype(F32)
    mu = jnp.mean(v, axis=-1, keepdims=True)
    var = jnp.mean(jnp.square(v - mu), axis=-1, keepdims=True)
    vn = ((v - mu) * lax.rsqrt(var + EPS) * lng_ref[0] + lnb_ref[0]).astype(BF16)
    for ch in range(v.shape[0] // MC):
        rs = slice(ch * MC, (ch + 1) * MC)
        for g in range(MG):
            cs = slice(g * 128, (g + 1) * 128)
            mixed = _dot(ws_ref[0, g], vn[rs, cs]) + bsb_ref[:, cs]
            lhs_scr[rs, cs] = (u_ref[rs, cs].astype(F32) * mixed).astype(BF16)
    return lhs_scr[...]


def _out_body(prologue, n_in, *refs):
    in_refs = refs[:n_in]
    wo_ref, x_ref, mod_ref, ng_ref, o_ref = refs[n_in:n_in + 5]
    lhs = prologue(*in_refs, *refs[n_in + 5:])
    y = _dot(lhs, wo_ref[0])
    o_ref[...] = x_ref[...] + mod_ref[0, 0, 2:3, :] * _rms(y, ng_ref[0, 1:2, :])


def _out_proj(prologue, ins, in_specs, wo, slot, S, mods, norm_g, layer, *, skip_ctx, scratch=(),
              name="out_proj"):
    tm = TM_OUT
    off = RC // tm if skip_ctx else 0
    rows = RX if skip_ctx else R
    specs = [pl.BlockSpec(bs, functools.partial(lambda i, f: f(i + off), f=f)) for bs, f in in_specs]
    specs += [
        pl.BlockSpec((1, D, D), lambda i: (slot, 0, 0)),
        pl.BlockSpec((tm, D), lambda i: (i + off, 0)),
        pl.BlockSpec((1, 1, 6, D), lambda i: (layer, _mod_row(i + off, tm), 0, 0)),
        pl.BlockSpec((1, 4, D), lambda i: (layer, 0, 0)),
    ]
    return pl.pallas_call(
        functools.partial(_out_body, prologue, len(ins)),
        grid=(rows // tm,),
        in_specs=specs,
        out_specs=pl.BlockSpec((tm, D), lambda i: (i, 0)),
        out_shape=jax.ShapeDtypeStruct((rows, D), F32),
        scratch_shapes=list(scratch),
        compiler_params=_cparams(("parallel",)),
        name=name,
    )(*ins, wo, S, mods, norm_g)


def _ffn_body(nf, x_ref, mod_ref, ng_ref, w1_ref, w3_ref, w2_ref, o_ref, h_scr):
    f = pl.program_id(1)

    @pl.when(f == 0)
    def _():
        h = _norm_mod(x_ref[...], ng_ref[0, 2:3, :], mod_ref[0, 0, 4:5, :], mod_ref[0, 0, 3:4, :])
        h_scr[...] = h.astype(BF16)
        o_ref[...] = jnp.zeros_like(o_ref)

    hb = h_scr[...]
    a = _dot(hb, w1_ref[0])
    b = _dot(hb, w3_ref[0])
    o_ref[...] += _dot((_silu(a) * b).astype(BF16), w2_ref[0])

    @pl.when(f == nf - 1)
    def _():
        o_ref[...] = x_ref[...] + mod_ref[0, 0, 5:6, :] * _rms(o_ref[...], ng_ref[0, 3:4, :])


def _ffn(S, mods, norm_g, w1, w3, w2, layer):
    tm, tf = TM_FFN, TF_FFN
    nf = FF // tf
    off = (R - S.shape[0]) // tm
    once = pl.Buffered(1)
    return pl.pallas_call(
        functools.partial(_ffn_body, nf),
        grid=(S.shape[0] // tm, nf),
        in_specs=[
            pl.BlockSpec((tm, D), lambda i, f: (i, 0), pipeline_mode=once),
            pl.BlockSpec((1, 1, 6, D), lambda i, f: (layer, _mod_row(i + off, tm), 0, 0)),
            pl.BlockSpec((1, 4, D), lambda i, f: (layer, 0, 0)),
            pl.BlockSpec((1, D, tf), lambda i, f: (layer, 0, f)),
            pl.BlockSpec((1, D, tf), lambda i, f: (layer, 0, f)),
            pl.BlockSpec((1, tf, D), lambda i, f: (layer, f, 0)),
        ],
        out_specs=pl.BlockSpec((tm, D), lambda i, f: (i, 0), pipeline_mode=once),
        out_shape=jax.ShapeDtypeStruct(S.shape, F32),
        scratch_shapes=[pltpu.VMEM((tm, D), BF16)],
        compiler_params=_cparams(("parallel", "arbitrary")),
        name="ffn",
    )(S, mods, norm_g, w1, w3, w2)


def _rope_tables():
    quarter = HD // 4
    inv_freq = 10000.0 ** (-jnp.arange(quarter, dtype=F32) / quarter)
    row = jnp.repeat(jnp.arange(L // 64), 64).astype(F32)
    col = jnp.tile(jnp.arange(64), L // 64).astype(F32)
    ang_r = row[:, None] * inv_freq
    ang_c = col[:, None] * inv_freq
    ang = jnp.concatenate([ang_r, ang_r, ang_c, ang_c], axis=-1)
    reps = TN_PROJ // HD
    return jnp.tile(jnp.cos(ang), (1, reps)), jnp.tile(jnp.sin(ang), (1, reps))


def kernel(x, c, ctx, c_ctx, ada_w, ada_b, norm_g, ffn_w1, ffn_w3, ffn_w2,
           gla_w_in, gla_wa2, gla_ba, gla_onorm_g, gla_wo,
           attn_w_in, attn_sink, attn_wo,
           gmlp_w_in, gmlp_ln_g, gmlp_ln_b, gmlp_ws, gmlp_bs, gmlp_wo):
    S = jnp.concatenate([ctx.reshape(RC, D), x.reshape(RX, D)], axis=0)
    cvec = jnp.concatenate([c_ctx[None, :], c, jnp.zeros((3, D), F32)], axis=0)
    mods = _modulation(cvec, ada_w, ada_b).reshape(DEPTH, 8, 6, D)
    rowtile = lambda i: (i, 0)
    w1, w3, w2 = ffn_w1.astype(BF16), ffn_w3.astype(BF16), ffn_w2.astype(BF16)
    gla_w, gla_wo_b = gla_w_in.astype(BF16), gla_wo.astype(BF16)
    hk, hv = GH * GDK, GH * GDV
    gla_wa = jnp.pad(gla_w_in[:, :, 2 * hk + 2 * hv:], ((0, 0), (0, 0), (0, 128 - 2 * GRANK))).astype(BF16)

    for i in range(DEPTH):
        last = i == DEPTH - 1
        kind, slot = i % 3, i // 3
        if kind == 0:
            p, a = _proj(S, mods, norm_g, i, gla_w, slot, 2 * hk + 2 * hv, w_small=gla_wa, name="gla_proj")
            wa2 = gla_wa2[slot]
            wa2p = jnp.zeros((2, 128, hk), F32)
            wa2p = wa2p.at[0, :GRANK].set(wa2[0]).at[1, GRANK:2 * GRANK].set(wa2[1]).astype(BF16)
            amat, masks = _gla_constants()
            o_f, o_b = _gla_scan(p, a, wa2p, gla_ba, slot, amat, masks)
            ins = [o_f, o_b, p, gla_onorm_g.reshape(-1, 1, hv)]
            in_specs = [((TM_OUT, hv), rowtile), ((TM_OUT, hv), rowtile),
                        ((TM_OUT, hv), lambda i: (i, 2)),
                        ((1, 1, hv), functools.partial(lambda i, s: (s, 0, 0), s=slot))]
            S = _out_proj(_gla_prologue, ins, in_specs, gla_wo_b, slot, S, mods, norm_g, i,
                          skip_ctx=last, name="gla_out")
        elif kind == 1:
            cos, sin = _rope_tables()
            p = _proj(S, mods, norm_g, i, attn_w_in.astype(BF16), slot, NQ + 2 * NKVD, kind="rope",
                      cos=cos, sin=sin, name="swa_proj")
            o = _swa_attention(p, attn_sink[slot:slot + 1])
            S = _out_proj(_swa_prologue, [o], [((TM_OUT, NQ), rowtile)], attn_wo.astype(BF16), slot,
                          S, mods, norm_g, i, skip_ctx=last, name="swa_out")
        else:
            p = _proj(S, mods, norm_g, i, gmlp_w_in.astype(BF16), slot, 2 * D, kind="gelu",
                      name="gmlp_proj")
            bsb = jnp.repeat(gmlp_bs[slot].T, 128, axis=1)
            sel = functools.partial(lambda i, s: (s, 0, 0), s=slot)
            ins = [p, p, gmlp_ln_g.reshape(-1, 1, D), gmlp_ln_b.reshape(-1, 1, D),
                   gmlp_ws.astype(BF16), bsb]
            in_specs = [((TM_OUT, D), rowtile), ((TM_OUT, D), lambda i: (i, 1)),
                        ((1, 1, D), sel), ((1, 1, D), sel),
                        ((1, MG, MC, MC), functools.partial(lambda i, s: (s, 0, 0, 0), s=slot)),
                        ((MC, D), lambda i: (0, 0))]
            S = _out_proj(_gmlp_prologue, ins, in_specs, gmlp_wo.astype(BF16), slot, S, mods, norm_g, i,
                          skip_ctx=last, scratch=[pltpu.VMEM((TM_OUT, D), BF16)], name="gmlp_out")
        S = _ffn(S, mods, norm_g, w1, w3, w2, i)
    return S.reshape(B, L, D)
```

```python
import functools

import numpy as np
import jax
import jax.numpy as jnp
from jax import lax
from jax.experimental import pallas as pl
from jax.experimental.pallas import tpu as pltpu

F32 = jnp.float32
BF16 = jnp.bfloat16

D = 2048
B = 4
L = 2048
LC = 256
DEPTH = 4
FF = 5632
EPS = 1e-6
NEG_INF = -1e30
LOG2E = 1.4426950408889634

RC = B * LC
RX = B * L
R = RC + RX

GH = 4
GDK = 256
GDV = 512
GRANK = 16
GCH = 128
GN_CH = (LC + L) // GCH
GC_CH = LC // GCH
G_LEVELS = 7
G_SMALL = 3
G_ROWS = (1 + G_SMALL) * GCH

HD = 64
NH = 32
NKV = 4
GRP = 8
WIN = 128
QB = 128
NQ = NH * HD
NKVD = NKV * HD

MC = 128
MG = 16

TM_PROJ = 1024
TN_PROJ = 512
TM_OUT = 512
TM_FFN = 1024
TF_FFN = 256

VMEM_LIMIT = 56 * 1024 * 1024


def _cparams(sem, vmem=VMEM_LIMIT):
    return pltpu.CompilerParams(dimension_semantics=sem, vmem_limit_bytes=vmem)


def _mod_row(i, tm):
    nct = RC // tm
    return jnp.where(i < nct, 0, 1 + (i - nct) // (L // tm))


def _norm_mod(x, ng, sc, sh):
    ms = jnp.mean(x * x, axis=-1, keepdims=True)
    return (x * lax.rsqrt(ms + EPS) * ng) * (1.0 + sc) + sh


def _rms(y, g):
    ms = jnp.mean(y * y, axis=-1, keepdims=True)
    return y * lax.rsqrt(ms + EPS) * g


ROW_CHUNK = 32


def _for_row_chunks(n_rows, fn, unroll=4):
    def body(r, carry):
        fn(pl.ds(pl.multiple_of(r * ROW_CHUNK, ROW_CHUNK), ROW_CHUNK))
        return carry
    lax.fori_loop(0, n_rows // ROW_CHUNK, body, 0, unroll=unroll)


def _row_rsqrt(src_ref, rs_scr):
    def body(rows):
        x = src_ref[rows, :]
        ms = jnp.mean(x * x, axis=-1, keepdims=True)
        rs_scr[rows, :] = jnp.broadcast_to(lax.rsqrt(ms + EPS), (ROW_CHUNK, 128))
    _for_row_chunks(src_ref.shape[0], body, unroll=8)


def _lanes(rs, width):
    return jnp.concatenate([rs] * (width // 128), axis=1)


def _silu(a):
    return a * jax.nn.sigmoid(a)


def _dot(a, b):
    return jnp.dot(a, b, preferred_element_type=F32)


def _dot_nt(a, b):
    return lax.dot_general(a, b, (((1,), (1,)), ((), ())), preferred_element_type=F32)


def _dot_tn(a, b):
    return lax.dot_general(a, b, (((0,), (0,)), ((), ())), preferred_element_type=F32)


def _mod_body(c_ref, w_ref, b_ref, o_ref):
    s = _silu(c_ref[...]).astype(BF16)
    o_ref[0] = _dot(s, w_ref[0].astype(BF16)) + b_ref[0]


def _modulation(cvec, ada_w, ada_b):
    tn = 1024
    n6 = 6 * D
    return pl.pallas_call(
        _mod_body,
        grid=(DEPTH, n6 // tn),
        in_specs=[
            pl.BlockSpec((8, D), lambda l, n: (0, 0)),
            pl.BlockSpec((1, D, tn), lambda l, n: (l, 0, n)),
            pl.BlockSpec((1, 1, tn), lambda l, n: (l, 0, n)),
        ],
        out_specs=pl.BlockSpec((1, 8, tn), lambda l, n: (l, 0, n)),
        out_shape=jax.ShapeDtypeStruct((DEPTH, 8, n6), F32),
        compiler_params=_cparams(("parallel", "parallel")),
        name="adaln_mod",
    )(cvec, ada_w, ada_b.reshape(DEPTH, 1, n6))


def _proj_body(kind, has_small, tm, tn, x_ref, mod_ref, ng_ref, w_ref, *rest):
    rest = list(rest)
    if kind == "rope":
        cos_ref, sin_ref = rest.pop(0), rest.pop(0)
    if has_small:
        ws_ref = rest.pop(0)
    o_ref = rest.pop(0)
    if has_small:
        os_ref = rest.pop(0)
    h_scr, rs_scr = rest.pop(0), rest.pop(0)
    i = pl.program_id(0)
    n = pl.program_id(1)

    @pl.when(n == 0)
    def _():
        _row_rsqrt(x_ref, rs_scr)
        gsc = ng_ref[0, 0:1, :] * (1.0 + mod_ref[0, 0, 1:2, :])

        def pro(rows):
            h = x_ref[rows, :] * _lanes(rs_scr[rows, :], D) * gsc + mod_ref[0, 0, 0:1, :]
            h_scr[rows, :] = h.astype(BF16)
        _for_row_chunks(tm, pro)
        if has_small:
            os_ref[...] = _dot(h_scr[...], ws_ref[0]).astype(os_ref.dtype)

    acc = _dot(h_scr[...], w_ref[0].astype(BF16))
    if kind == "none":
        o_ref[...] = acc.astype(o_ref.dtype)
    elif kind == "gelu":
        o_ref[...] = jax.nn.gelu(acc, approximate=True).astype(o_ref.dtype)
    else:
        lane = lax.broadcasted_iota(jnp.int32, acc.shape, 1)
        first = (lane % 32) < 16
        rot = jnp.where(first, -pltpu.roll(acc, tn - 16, 1), pltpu.roll(acc, 16, 1))
        roped = acc * cos_ref[...] + rot * sin_ref[...]
        rope_cols = jnp.where(i >= RC // tm, NQ + NKVD, 0)
        scale = jnp.where(n < NQ // tn, HD ** -0.5, 1.0).astype(F32)
        o_ref[...] = (jnp.where(n * tn + lane < rope_cols, roped, acc) * scale).astype(o_ref.dtype)


def _proj(S, mods, norm_g, layer, w, slot, n_cols, *, kind="none", w_small=None, cos=None, sin=None,
          name="proj"):
    tm, tn = TM_PROJ, TN_PROJ
    nct = RC // tm
    has_small = w_small is not None
    in_specs = [
        pl.BlockSpec((tm, D), lambda i, n: (i, 0), pipeline_mode=pl.Buffered(1)),
        pl.BlockSpec((1, 1, 6, D), lambda i, n: (layer, _mod_row(i, tm), 0, 0)),
        pl.BlockSpec((1, 4, D), lambda i, n: (layer, 0, 0)),
        pl.BlockSpec((1, D, tn), lambda i, n: (slot, 0, n)),
    ]
    args = [S, mods, norm_g, w]
    if kind == "rope":
        pos = lambda i, n: (jnp.where(i < nct, 0, (i - nct) % (L // tm)), 0)
        in_specs += [pl.BlockSpec((tm, tn), pos), pl.BlockSpec((tm, tn), pos)]
        args += [cos, sin]
    out_specs = [pl.BlockSpec((tm, tn), lambda i, n: (i, n))]
    out_shape = [jax.ShapeDtypeStruct((R, n_cols), BF16)]
    if has_small:
        ns = w_small.shape[2]
        in_specs.append(pl.BlockSpec((1, D, ns), lambda i, n: (slot, 0, 0)))
        args.append(w_small)
        out_specs.append(pl.BlockSpec((tm, ns), lambda i, n: (i, 0)))
        out_shape.append(jax.ShapeDtypeStruct((R, ns), BF16))
    res = pl.pallas_call(
        functools.partial(_proj_body, kind, has_small, tm, tn),
        grid=(R // tm, n_cols // tn),
        in_specs=in_specs,
        out_specs=out_specs,
        out_shape=out_shape,
        scratch_shapes=[pltpu.VMEM((tm, D), BF16), pltpu.VMEM((tm, 128), F32)],
        compiler_params=_cparams(("parallel", "arbitrary")),
        name=name,
    )(*args)
    return res if has_small else res[0]


def _gla_constants():
    c = GCH
    r = np.arange(c)[:, None]
    t = np.arange(c)[None, :]
    blocks = [(t <= r)]
    masks = []
    for lv in range(G_LEVELS):
        s = c >> (lv + 1)
        m = r // s
        odd = (m % 2) == 1
        if s < 8:
            a_odd = (t > s * m) & (t <= r)
            a_even = (t > r) & (t <= s * (m + 1))
            blocks.append(np.where(odd, a_odd, a_even))
        masks.append(odd & ((t // s) == m - 1))
    masks.append(r == t)
    a_f = np.concatenate(blocks, axis=0).astype(np.float32)
    m_f = np.stack(masks).astype(np.float32)
    a_b = np.concatenate([blk[::-1, ::-1] for blk in blocks], axis=0).astype(np.float32)
    m_b = m_f[:, ::-1, ::-1]
    amat = np.stack([a_f, a_b])
    amat = np.concatenate([amat, amat], axis=2)
    return jnp.asarray(amat, BF16), jnp.asarray(np.stack([m_f, m_b]), F32)


def _gla_body(qf, kf, vf, af, qb, kb, vb, ab, wa_ref, ba_ref, a_ref, m_ref, of_ref, ob_ref, s_scr):
    c = pl.program_id(1)

    @pl.when(c == 0)
    def _():
        s_scr[...] = jnp.zeros_like(s_scr)

    row = lax.broadcasted_iota(jnp.int32, (GCH, GDK), 0)
    dirs = ((qf, kf, vf, af, of_ref, GCH - 1), (qb, kb, vb, ab, ob_ref, 0))
    for d, (q_ref, k_ref, v_ref, ar_ref, o_ref, last) in enumerate(dirs):
        araw = _dot(ar_ref[...], wa_ref[d]) + ba_ref[d:d + 1, :]
        g = (jnp.minimum(araw, 0.0) - jnp.log1p(jnp.exp(-jnp.abs(araw)))) * (LOG2E / 16.0)
        g1 = g.astype(BF16)
        g2 = (g - g1.astype(F32)).astype(BF16)
        ex_mm = _dot(a_ref[d], jnp.concatenate([g1, g2], axis=0))
        bcum = ex_mm[0:GCH]

        def big_level(s):
            parts = []
            for p in range(GCH // (2 * s)):
                lo = 2 * p * s
                ev, od = bcum[lo:lo + s], bcum[lo + s:lo + 2 * s]
                if d == 0:
                    ref = bcum[lo + s:lo + s + 1]
                    parts += [ref - ev, od - ref]
                else:
                    ref = bcum[lo + s - 1:lo + s]
                    parts += [ev - ref, ref - od]
            return jnp.concatenate(parts, axis=0)

        e_cum_all = jnp.exp2(bcum)
        e_rem_all = jnp.exp2((bcum[GCH - 1:GCH] if d == 0 else bcum[0:1]) - bcum)
        e_lv = []
        for lv in range(G_LEVELS):
            s = GCH >> (lv + 1)
            n_big = G_LEVELS - G_SMALL
            ex = big_level(s) if s >= 8 else ex_mm[(1 + lv - n_big) * GCH:(2 + lv - n_big) * GCH]
            e_lv.append(jnp.exp2(ex))

        def zsel(q, k, lv):
            s = GCH >> (lv + 1)
            if s >= 8:
                parts = [(q if (m % 2 == 1) == (d == 0) else k)[m * s:(m + 1) * s]
                         for m in range(GCH // s)]
                return jnp.concatenate(parts, axis=0)
            odd = ((row // s) % 2) == 1
            return jnp.where(odd, q, k) if d == 0 else jnp.where(odd, k, q)

        for h in range(GH):
            sl = slice(h * GDK, (h + 1) * GDK)
            vs = slice(h * GDV, (h + 1) * GDV)
            q = q_ref[:, sl].astype(F32) * (GDK ** -0.5)
            k = k_ref[:, sl].astype(F32)
            v = v_ref[:, vs]
            e_cum = e_cum_all[:, sl]
            e_rem = e_rem_all[:, sl]
            st = s_scr[d, h]
            o = _dot_nt((q * e_cum).astype(BF16), st.astype(BF16))
            att = m_ref[d, G_LEVELS] * _dot_nt(q.astype(BF16), k.astype(BF16))
            for lv in range(G_LEVELS):
                z = (zsel(q, k, lv) * e_lv[lv][:, sl]).astype(BF16)
                att = att + m_ref[d, lv] * _dot_nt(z, z)
            o = o + _dot(att.astype(BF16), v)
            o_ref[:, vs] = o
            s_scr[d, h] = st * e_cum[last:last + 1, :] + _dot_tn(v, (k * e_rem).astype(BF16))


def _gla_scan(p, a, wa2p, ba, slot, amat, masks):
    def rowblk(b, m):
        return jnp.where(m < GC_CH, b * GC_CH + m, RC // GCH + b * (L // GCH) + (m - GC_CH))

    def bwd(c):
        return jnp.where(c < GC_CH, GC_CH - 1 - c, GN_CH + GC_CH - 1 - c)

    hk = GH * GDK
    hv = GH * GDV

    def specs(order):
        return [
            pl.BlockSpec((GCH, hk), lambda b, c: (rowblk(b, order(c)), 0)),
            pl.BlockSpec((GCH, hk), lambda b, c: (rowblk(b, order(c)), 1)),
            pl.BlockSpec((GCH, hv), lambda b, c: (rowblk(b, order(c)), 1)),
            pl.BlockSpec((GCH, 128), lambda b, c: (rowblk(b, order(c)), 0)),
        ]

    in_specs = specs(lambda c: c) + specs(bwd) + [
        pl.BlockSpec((2, 128, hk), lambda b, c: (0, 0, 0)),
        pl.BlockSpec((1, 2, hk), lambda b, c: (slot, 0, 0)),
        pl.BlockSpec((2, G_ROWS, 2 * GCH), lambda b, c: (0, 0, 0)),
        pl.BlockSpec((2, G_LEVELS + 1, GCH, GCH), lambda b, c: (0, 0, 0, 0)),
    ]
    out_specs = [
        pl.BlockSpec((GCH, hv), lambda b, c: (rowblk(b, c), 0)),
        pl.BlockSpec((GCH, hv), lambda b, c: (rowblk(b, bwd(c)), 0)),
    ]

    def body(qf, kf, vf, af, qb, kb, vb, ab, wa_ref, ba_ref, a_ref, m_ref, of_ref, ob_ref, s_scr):
        _gla_body(qf, kf, vf, af, qb, kb, vb, ab, wa_ref, ba_ref.at[0], a_ref, m_ref, of_ref, ob_ref, s_scr)

    return pl.pallas_call(
        body,
        grid=(B, GN_CH),
        in_specs=in_specs,
        out_specs=out_specs,
        out_shape=[jax.ShapeDtypeStruct((R, hv), F32)] * 2,
        scratch_shapes=[pltpu.VMEM((2, GH, GDV, GDK), F32)],
        compiler_params=_cparams(("parallel", "arbitrary")),
        name="gla_scan",
    )(p, p, p, a, p, p, p, a, wa2p, ba, amat, masks)


def _swa_body(sink_ref, q_ref, kc_ref, vc_ref, k0, k1, k2, v0, v1, v2, o_ref):
    j = pl.program_id(1)
    is_ctx = j < LC // QB
    jl = j - LC // QB
    lo = jnp.where(is_ctx, 0, jnp.where(jl >= 1, 0, QB))
    hi = jnp.where(is_ctx, 0, jnp.where(jl <= L // QB - 2, 3 * QB, 2 * QB))
    nk = LC + 3 * QB
    rr = lax.broadcasted_iota(jnp.int32, (QB, nk), 0)
    tt = lax.broadcasted_iota(jnp.int32, (QB, nk), 1) - LC
    dlt = tt - rr
    valid = (tt < 0) | ((dlt >= 0) & (dlt <= 2 * WIN) & (tt >= lo) & (tt < hi))
    bias = jnp.where(valid, 0.0, NEG_INF)

    kcat = jnp.concatenate([kc_ref[...], k0[...], k1[...], k2[...]], axis=0).astype(F32)
    vcat = jnp.concatenate([vc_ref[...], v0[...], v1[...], v2[...]], axis=0).astype(F32)
    lane_q = lax.broadcasted_iota(jnp.int32, (QB, 128), 1)

    def dup(xp, half):
        low = lax.broadcasted_iota(jnp.int32, xp.shape, 1) < HD
        rolled = pltpu.roll(xp, HD, 1)
        out = jnp.where(low, xp, rolled) if half == 0 else jnp.where(low, rolled, xp)
        return out.astype(BF16)

    for kh in range(NKV):
        ps = slice((kh // 2) * 128, (kh // 2 + 1) * 128)
        kd, vd = dup(kcat[:, ps], kh % 2), dup(vcat[:, ps], kh % 2)
        qs = []
        for g in range(GRP):
            h = kh * GRP + g
            blk = q_ref[:, (h // 2) * 128:(h // 2 + 1) * 128].astype(F32)
            keep = (lane_q < HD) if h % 2 == 0 else (lane_q >= HD)
            qs.append(jnp.where(keep, blk, 0.0).astype(BF16))
        qst = jnp.concatenate(qs, axis=0)
        logits = _dot_nt(qst, kd)
        es, invs = [], []
        for g in range(GRP):
            lg = logits[g * QB:(g + 1) * QB] + bias
            s = sink_ref[0, kh * GRP + g]
            m = jnp.maximum(jnp.max(lg, axis=-1, keepdims=True), s)
            e = jnp.exp(lg - m)
            den = jnp.sum(e, axis=-1, keepdims=True) + jnp.exp(s - m)
            es.append(e.astype(BF16))
            invs.append(1.0 / den)
        res = _dot(jnp.concatenate(es, axis=0), vd) * jnp.concatenate(invs, axis=0)
        for m2 in range(GRP // 2):
            even = res[(2 * m2) * QB:(2 * m2 + 1) * QB]
            odd = res[(2 * m2 + 1) * QB:(2 * m2 + 2) * QB]
            col = (kh * (GRP // 2) + m2) * 128
            o_ref[:, col:col + 128] = jnp.where(lane_q < HD, even, odd).astype(o_ref.dtype)


def _swa_attention(p, sink):
    ncb = LC // QB
    nlb = L // QB
    lat0 = RC // QB
    kcol = NQ // NKVD

    def qrow(b, j):
        return jnp.where(j < ncb, b * ncb + j, lat0 + b * nlb + (j - ncb))

    def wrow(off):
        def f(b, j):
            jl = jnp.clip(j - ncb + off, 0, nlb - 1)
            return lat0 + b * nlb + jl
        return f

    in_specs = [
        pl.BlockSpec(memory_space=pltpu.SMEM),
        pl.BlockSpec((QB, NQ), lambda b, j: (qrow(b, j), 0)),
        pl.BlockSpec((LC, NKVD), lambda b, j: (b, kcol)),
        pl.BlockSpec((LC, NKVD), lambda b, j: (b, kcol + 1)),
    ]
    for col in (kcol, kcol + 1):
        for off in (-1, 0, 1):
            in_specs.append(pl.BlockSpec((QB, NKVD), functools.partial(
                lambda b, j, f, cc: (f(b, j), cc), f=wrow(off), cc=col)))
    return pl.pallas_call(
        _swa_body,
        grid=(B, ncb + nlb),
        in_specs=in_specs,
        out_specs=pl.BlockSpec((QB, NQ), lambda b, j: (qrow(b, j), 0)),
        out_shape=jax.ShapeDtypeStruct((R, NQ), BF16),
        compiler_params=_cparams(("parallel", "parallel")),
        name="swa_attn",
    )(sink, p, p, p, p, p, p, p, p, p)


def _gla_prologue(of_ref, ob_ref, og_ref, on_ref):
    o = of_ref[...] + ob_ref[...]
    parts = [_rms(o[:, h * GDV:(h + 1) * GDV], on_ref[0, :, h * GDV:(h + 1) * GDV]) for h in range(GH)]
    return (jnp.concatenate(parts, axis=-1) * _silu(og_ref[...].astype(F32))).astype(BF16)


def _swa_prologue(o_ref):
    return o_ref[...]


def _gmlp_prologue(u_ref, v_ref, lng_ref, lnb_ref, ws_ref, bsb_ref, lhs_scr):
    v = v_ref[...].astype(F32)
    mu = jnp.mean(v, axis=-1, keepdims=True)
    var = jnp.mean(jnp.square(v - mu), axis=-1, keepdims=True)
    vn = ((v - mu) * lax.rsqrt(var + EPS) * lng_ref[0] + lnb_ref[0]).astype(BF16)
    for ch in range(v.shape[0] // MC):
        rs = slice(ch * MC, (ch + 1) * MC)
        for g in range(MG):
            cs = slice(g * 128, (g + 1) * 128)
            mixed = _dot(ws_ref[0, g], vn[rs, cs]) + bsb_ref[:, cs]
            lhs_scr[rs, cs] = (u_ref[rs, cs].astype(F32) * mixed).astype(BF16)
    return lhs_scr[...]


def _out_body(prologue, n_in, *refs):
    in_refs = refs[:n_in]
    wo_ref, x_ref, mod_ref, ng_ref, o_ref = refs[n_in:n_in + 5]
    lhs = prologue(*in_refs, *refs[n_in + 5:])
    y = _dot(lhs, wo_ref[0])
    o_ref[...] = x_ref[...] + mod_ref[0, 0, 2:3, :] * _rms(y, ng_ref[0, 1:2, :])


def _out_proj(prologue, ins, in_specs, wo, slot, S, mods, norm_g, layer, *, skip_ctx, scratch=(),
              name="out_proj"):
    tm = TM_OUT
    off = RC // tm if skip_ctx else 0
    rows = RX if skip_ctx else R
    specs = [pl.BlockSpec(bs, functools.partial(lambda i, f: f(i + off), f=f)) for bs, f in in_specs]
    specs += [
        pl.BlockSpec((1, D, D), lambda i: (slot, 0, 0)),
        pl.BlockSpec((tm, D), lambda i: (i + off, 0)),
        pl.BlockSpec((1, 1, 6, D), lambda i: (layer, _mod_row(i + off, tm), 0, 0)),
        pl.BlockSpec((1, 4, D), lambda i: (layer, 0, 0)),
    ]
    return pl.pallas_call(
        functools.partial(_out_body, prologue, len(ins)),
        grid=(rows // tm,),
        in_specs=specs,
        out_specs=pl.BlockSpec((tm, D), lambda i: (i, 0)),
        out_shape=jax.ShapeDtypeStruct((rows, D), F32),
        scratch_shapes=list(scratch),
        compiler_params=_cparams(("parallel",)),
        name=name,
    )(*ins, wo, S, mods, norm_g)


def _ffn_body(nf, x_ref, mod_ref, ng_ref, w1_ref, w3_ref, w2_ref, o_ref, h_scr, rs_scr):
    f = pl.program_id(1)

    @pl.when(f == 0)
    def _():
        _row_rsqrt(x_ref, rs_scr)
        gsc = ng_ref[0, 2:3, :] * (1.0 + mod_ref[0, 0, 4:5, :])

        def pro(rows):
            h = x_ref[rows, :] * _lanes(rs_scr[rows, :], D) * gsc + mod_ref[0, 0, 3:4, :]
            h_scr[rows, :] = h.astype(BF16)
            o_ref[rows, :] = jnp.zeros((ROW_CHUNK, D), F32)
        _for_row_chunks(x_ref.shape[0], pro)

    hb = h_scr[...]
    a = _dot(hb, w1_ref[0].astype(BF16))
    b = _dot(hb, w3_ref[0].astype(BF16))
    o_ref[...] += _dot((_silu(a) * b).astype(BF16), w2_ref[0].astype(BF16))

    @pl.when(f == nf - 1)
    def _():
        _row_rsqrt(o_ref, rs_scr)
        gg = mod_ref[0, 0, 5:6, :] * ng_ref[0, 3:4, :]

        def epi(rows):
            o_ref[rows, :] = x_ref[rows, :] + o_ref[rows, :] * _lanes(rs_scr[rows, :], D) * gg
        _for_row_chunks(x_ref.shape[0], epi)


def _ffn(S, mods, norm_g, w1, w3, w2, layer):
    tm, tf = TM_FFN, TF_FFN
    nf = FF // tf
    off = (R - S.shape[0]) // tm
    return pl.pallas_call(
        functools.partial(_ffn_body, nf),
        grid=(S.shape[0] // tm, nf),
        in_specs=[
            pl.BlockSpec((tm, D), lambda i, f: (i, 0)),
            pl.BlockSpec((1, 1, 6, D), lambda i, f: (layer, _mod_row(i + off, tm), 0, 0)),
            pl.BlockSpec((1, 4, D), lambda i, f: (layer, 0, 0)),
            pl.BlockSpec((1, D, tf), lambda i, f: (layer, 0, f)),
            pl.BlockSpec((1, D, tf), lambda i, f: (layer, 0, f)),
            pl.BlockSpec((1, tf, D), lambda i, f: (layer, f, 0)),
        ],
        out_specs=pl.BlockSpec((tm, D), lambda i, f: (i, 0)),
        out_shape=jax.ShapeDtypeStruct(S.shape, F32),
        scratch_shapes=[pltpu.VMEM((tm, D), BF16), pltpu.VMEM((tm, 128), F32)],
        compiler_params=_cparams(("parallel", "arbitrary")),
        name="ffn",
    )(S, mods, norm_g, w1, w3, w2)


def _rope_tables():
    quarter = HD // 4
    inv_freq = 10000.0 ** (-jnp.arange(quarter, dtype=F32) / quarter)
    row = jnp.repeat(jnp.arange(L // 64), 64).astype(F32)
    col = jnp.tile(jnp.arange(64), L // 64).astype(F32)
    ang_r = row[:, None] * inv_freq
    ang_c = col[:, None] * inv_freq
    ang = jnp.concatenate([ang_r, ang_r, ang_c, ang_c], axis=-1)
    reps = TN_PROJ // HD
    return jnp.tile(jnp.cos(ang), (1, reps)), jnp.tile(jnp.sin(ang), (1, reps))


def kernel(x, c, ctx, c_ctx, ada_w, ada_b, norm_g, ffn_w1, ffn_w3, ffn_w2,
           gla_w_in, gla_wa2, gla_ba, gla_onorm_g, gla_wo,
           attn_w_in, attn_sink, attn_wo,
           gmlp_w_in, gmlp_ln_g, gmlp_ln_b, gmlp_ws, gmlp_bs, gmlp_wo):
    S = jnp.concatenate([ctx.reshape(RC, D), x.reshape(RX, D)], axis=0)
    cvec = jnp.concatenate([c_ctx[None, :], c, jnp.zeros((3, D), F32)], axis=0)
    mods = _modulation(cvec, ada_w, ada_b).reshape(DEPTH, 8, 6, D)
    rowtile = lambda i: (i, 0)
    w1, w3, w2 = ffn_w1, ffn_w3, ffn_w2
    gla_w, gla_wo_b = gla_w_in, gla_wo.astype(BF16)
    hk, hv = GH * GDK, GH * GDV
    gla_wa = jnp.pad(gla_w_in[:, :, 2 * hk + 2 * hv:], ((0, 0), (0, 0), (0, 128 - 2 * GRANK))).astype(BF16)

    for i in range(DEPTH):
        last = i == DEPTH - 1
        kind, slot = i % 3, i // 3
        if kind == 0:
            p, a = _proj(S, mods, norm_g, i, gla_w, slot, 2 * hk + 2 * hv, w_small=gla_wa, name="gla_proj")
            wa2 = gla_wa2[slot]
            wa2p = jnp.zeros((2, 128, hk), F32)
            wa2p = wa2p.at[0, :GRANK].set(wa2[0]).at[1, GRANK:2 * GRANK].set(wa2[1]).astype(BF16)
            amat, masks = _gla_constants()
            o_f, o_b = _gla_scan(p, a, wa2p, gla_ba, slot, amat, masks)
            ins = [o_f, o_b, p, gla_onorm_g.reshape(-1, 1, hv)]
            in_specs = [((TM_OUT, hv), rowtile), ((TM_OUT, hv), rowtile),
                        ((TM_OUT, hv), lambda i: (i, 2)),
                        ((1, 1, hv), functools.partial(lambda i, s: (s, 0, 0), s=slot))]
            S = _out_proj(_gla_prologue, ins, in_specs, gla_wo_b, slot, S, mods, norm_g, i,
                          skip_ctx=last, name="gla_out")
        elif kind == 1:
            cos, sin = _rope_tables()
            p = _proj(S, mods, norm_g, i, attn_w_in, slot, NQ + 2 * NKVD, kind="rope",
                      cos=cos, sin=sin, name="swa_proj")
            o = _swa_attention(p, attn_sink[slot:slot + 1])
            S = _out_proj(_swa_prologue, [o], [((TM_OUT, NQ), rowtile)], attn_wo.astype(BF16), slot,
                          S, mods, norm_g, i, skip_ctx=last, name="swa_out")
        else:
            p = _proj(S, mods, norm_g, i, gmlp_w_in, slot, 2 * D, kind="gelu",
                      name="gmlp_proj")
            bsb = jnp.repeat(gmlp_bs[slot].T, 128, axis=1)
            sel = functools.partial(lambda i, s: (s, 0, 0), s=slot)
            ins = [p, p, gmlp_ln_g.reshape(-1, 1, D), gmlp_ln_b.reshape(-1, 1, D),
                   gmlp_ws.astype(BF16), bsb]
            in_specs = [((TM_OUT, D), rowtile), ((TM_OUT, D), lambda i: (i, 1)),
                        ((1, 1, D), sel), ((1, 1, D), sel),
                        ((1, MG, MC, MC), functools.partial(lambda i, s: (s, 0, 0, 0), s=slot)),
                        ((MC, D), lambda i: (0, 0))]
            S = _out_proj(_gmlp_prologue, ins, in_specs, gmlp_wo.astype(BF16), slot, S, mods, norm_g, i,
                          skip_ctx=last, scratch=[pltpu.VMEM((TM_OUT, D), BF16)], name="gmlp_out")
        S = _ffn(S, mods, norm_g, w1, w3, w2, i)
    return S.reshape(B, L, D)
```

```python
import functools

import numpy as np
import jax
import jax.numpy as jnp
from jax import lax
from jax.experimental import pallas as pl
from jax.experimental.pallas import tpu as pltpu

F32 = jnp.float32
BF16 = jnp.bfloat16

D = 2048
B = 4
L = 2048
LC = 256
DEPTH = 4
FF = 5632
EPS = 1e-6
NEG_INF = -1e30
LOG2E = 1.4426950408889634

RC = B * LC
RX = B * L
R = RC + RX

GH = 4
GDK = 256
GDV = 512
GRANK = 16
GCH = 128
GN_CH = (LC + L) // GCH
GC_CH = LC // GCH
G_LEVELS = 7
G_SMALL = 3
G_ROWS = (1 + G_SMALL) * GCH

HD = 64
NH = 32
NKV = 4
GRP = 8
WIN = 128
QB = 128
NQ = NH * HD
NKVD = NKV * HD

MC = 128
MG = 16

TM_PROJ = 1024
TN_PROJ = 512
TM_OUT = 512
TM_FFN = 1024
TF_FFN = 256

VMEM_LIMIT = 56 * 1024 * 1024


def _cparams(sem, vmem=VMEM_LIMIT):
    return pltpu.CompilerParams(dimension_semantics=sem, vmem_limit_bytes=vmem)


def _mod_row(i, tm):
    nct = RC // tm
    return jnp.where(i < nct, 0, 1 + (i - nct) // (L // tm))


def _norm_mod(x, ng, sc, sh):
    ms = jnp.mean(x * x, axis=-1, keepdims=True)
    return (x * lax.rsqrt(ms + EPS) * ng) * (1.0 + sc) + sh


def _rms(y, g):
    ms = jnp.mean(y * y, axis=-1, keepdims=True)
    return y * lax.rsqrt(ms + EPS) * g


ROW_CHUNK = 32


def _for_row_chunks(n_rows, fn, unroll=4):
    def body(r, carry):
        fn(pl.ds(pl.multiple_of(r * ROW_CHUNK, ROW_CHUNK), ROW_CHUNK))
        return carry
    lax.fori_loop(0, n_rows // ROW_CHUNK, body, 0, unroll=unroll)


def _row_rsqrt(load, rs_scr):
    def body(rows):
        x = load(rows)
        ms = jnp.mean(x * x, axis=-1, keepdims=True)
        rs_scr[rows, :] = jnp.broadcast_to(lax.rsqrt(ms + EPS), (ROW_CHUNK, 128))
    _for_row_chunks(rs_scr.shape[0], body, unroll=8)


def _lanes(rs, width):
    return jnp.concatenate([rs] * (width // 128), axis=1)


def _silu(a):
    return a * jax.nn.sigmoid(a)


def _dot(a, b):
    return jnp.dot(a, b, preferred_element_type=F32)


def _dot_nt(a, b):
    return lax.dot_general(a, b, (((1,), (1,)), ((), ())), preferred_element_type=F32)


def _dot_tn(a, b):
    return lax.dot_general(a, b, (((0,), (0,)), ((), ())), preferred_element_type=F32)


def _mod_body(c_ref, w_ref, b_ref, o_ref):
    s = _silu(c_ref[...]).astype(BF16)
    o_ref[0] = _dot(s, w_ref[0].astype(BF16)) + b_ref[0]


def _modulation(cvec, ada_w, ada_b):
    tn = 1024
    n6 = 6 * D
    return pl.pallas_call(
        _mod_body,
        grid=(DEPTH, n6 // tn),
        in_specs=[
            pl.BlockSpec((8, D), lambda l, n: (0, 0)),
            pl.BlockSpec((1, D, tn), lambda l, n: (l, 0, n)),
            pl.BlockSpec((1, 1, tn), lambda l, n: (l, 0, n)),
        ],
        out_specs=pl.BlockSpec((1, 8, tn), lambda l, n: (l, 0, n)),
        out_shape=jax.ShapeDtypeStruct((DEPTH, 8, n6), F32),
        compiler_params=_cparams(("parallel", "parallel")),
        name="adaln_mod",
    )(cvec, ada_w, ada_b.reshape(DEPTH, 1, n6))


SUB_N = 256


def _two_source_rows(is_ctx, c_ref, x_ref):
    return lambda rows: jnp.where(is_ctx, c_ref[rows, :], x_ref[rows, :])


def _proj_body(kind, small_rows, w_t, dual, tm, tn, *refs):
    rest = list(refs)
    if dual:
        load_x = _two_source_rows(pl.program_id(0) < RC // tm, rest.pop(0), rest.pop(0))
    else:
        x_ref = rest.pop(0)
        load_x = lambda rows: x_ref[rows, :]
    mod_ref, ng_ref, w_ref = rest.pop(0), rest.pop(0), rest.pop(0)
    if kind == "rope":
        cos_ref, sin_ref = rest.pop(0), rest.pop(0)
    if small_rows:
        ws_ref = rest.pop(0)
    o_ref = rest.pop(0)
    if small_rows:
        os_ref = rest.pop(0)
    h_scr, rs_scr = rest.pop(0), rest.pop(0)
    i = pl.program_id(0)
    n = pl.program_id(1)

    @pl.when(n == 0)
    def _():
        _row_rsqrt(load_x, rs_scr)
        gsc = ng_ref[0, 0:1, :] * (1.0 + mod_ref[0, 0, 1:2, :])

        def pro(rows):
            h = load_x(rows) * _lanes(rs_scr[rows, :], D) * gsc + mod_ref[0, 0, 0:1, :]
            h_scr[rows, :] = h.astype(BF16)
        _for_row_chunks(tm, pro)
        if small_rows:
            os_ref[...] = _dot_nt(h_scr[...], ws_ref[0].astype(BF16)).astype(os_ref.dtype)

    hb = h_scr[...]
    for c in range(tn // SUB_N):
        cs = slice(c * SUB_N, (c + 1) * SUB_N)
        if w_t:
            acc = _dot_nt(hb, w_ref[0, cs, :].astype(BF16))
        else:
            acc = _dot(hb, w_ref[0, :, cs].astype(BF16))
        if kind == "none":
            o_ref[:, cs] = acc.astype(o_ref.dtype)
        elif kind == "gelu":
            o_ref[:, cs] = jax.nn.gelu(acc, approximate=True).astype(o_ref.dtype)
        else:
            piece = n * (tn // SUB_N) + c
            scale = jnp.where(piece < NQ // SUB_N, HD ** -0.5, 1.0).astype(F32)
            rotate = jnp.logical_and(i >= RC // tm, piece < (NQ + NKVD) // SUB_N)

            roped = (acc * cos_ref[...] + pltpu.roll(acc, SUB_N - 16, 1) * sin_ref[0]
                     + pltpu.roll(acc, 16, 1) * sin_ref[1])
            o_ref[:, cs] = (jnp.where(rotate, roped, acc) * scale).astype(o_ref.dtype)


def _proj(S, mods, norm_g, layer, w, slot, n_cols, *, kind="none", w_t=False, small_rows=0,
          cos=None, sin=None, name="proj"):
    tm, tn = TM_PROJ, TN_PROJ
    nct = RC // tm
    once = pl.Buffered(1)
    dual = isinstance(S, tuple)
    if dual:
        row_specs = [pl.BlockSpec((tm, D), lambda i, n: (jnp.minimum(i, nct - 1), 0), pipeline_mode=once),
                     pl.BlockSpec((tm, D), lambda i, n: (jnp.maximum(i - nct, 0), 0), pipeline_mode=once)]
    else:
        row_specs = [pl.BlockSpec((tm, D), lambda i, n: (i, 0), pipeline_mode=once)]
    in_specs = row_specs + [
        pl.BlockSpec((1, 1, 6, D), lambda i, n: (layer, _mod_row(i, tm), 0, 0)),
        pl.BlockSpec((1, 4, D), lambda i, n: (layer, 0, 0)),
        (pl.BlockSpec((1, tn, D), lambda i, n: (slot, n, 0)) if w_t
         else pl.BlockSpec((1, D, tn), lambda i, n: (slot, 0, n))),
    ]
    args = (list(S) if dual else [S]) + [mods, norm_g, w]
    if kind == "rope":
        pos = lambda i, n: jnp.where(i < nct, 0, (i - nct) % (L // tm))
        in_specs += [pl.BlockSpec((tm, SUB_N), lambda i, n: (pos(i, n), 0)),
                     pl.BlockSpec((2, tm, SUB_N), lambda i, n: (0, pos(i, n), 0))]
        args += [cos, sin]
    out_specs = [pl.BlockSpec((tm, tn), lambda i, n: (i, n))]
    out_shape = [jax.ShapeDtypeStruct((R, n_cols), BF16)]
    if small_rows:
        assert w_t and n_cols % small_rows == 0
        in_specs.append(pl.BlockSpec((1, small_rows, D), lambda i, n: (slot, n_cols // small_rows, 0)))
        args.append(w)
        out_specs.append(pl.BlockSpec((tm, small_rows), lambda i, n: (i, 0)))
        out_shape.append(jax.ShapeDtypeStruct((R, small_rows), BF16))
    res = pl.pallas_call(
        functools.partial(_proj_body, kind, small_rows, w_t, dual, tm, tn),
        grid=(R // tm, n_cols // tn),
        in_specs=in_specs,
        out_specs=out_specs,
        out_shape=out_shape,
        scratch_shapes=[pltpu.VMEM((tm, D), BF16), pltpu.VMEM((tm, 128), F32)],
        compiler_params=_cparams(("parallel", "arbitrary")),
        name=name,
    )(*args)
    return res if small_rows else res[0]


def _gla_constants():
    c = GCH
    r = np.arange(c)[:, None]
    t = np.arange(c)[None, :]
    blocks = [(t <= r)]
    masks = []
    for lv in range(G_LEVELS):
        s = c >> (lv + 1)
        m = r // s
        odd = (m % 2) == 1
        if s < 8:
            a_odd = (t > s * m) & (t <= r)
            a_even = (t > r) & (t <= s * (m + 1))
            blocks.append(np.where(odd, a_odd, a_even))
        masks.append(odd & ((t // s) == m - 1))
    masks.append(r == t)
    a_f = np.concatenate(blocks, axis=0).astype(np.float32)
    m_f = np.stack(masks).astype(np.float32)
    a_b = np.concatenate([blk[::-1, ::-1] for blk in blocks], axis=0).astype(np.float32)
    m_b = m_f[:, ::-1, ::-1]
    amat = np.stack([a_f, a_b])
    amat = np.concatenate([amat, amat], axis=2)
    return jnp.asarray(amat, BF16), jnp.asarray(np.stack([m_f, m_b]), F32)


def _gla_body(qf, kf, vf, af, qb, kb, vb, ab, wa_ref, ba_ref, a_ref, m_ref, of_ref, ob_ref, s_scr):
    c = pl.program_id(1)

    @pl.when(c == 0)
    def _():
        s_scr[...] = jnp.zeros_like(s_scr)

    row = lax.broadcasted_iota(jnp.int32, (GCH, GDK), 0)
    dirs = ((qf, kf, vf, af, of_ref, GCH - 1), (qb, kb, vb, ab, ob_ref, 0))
    for d, (q_ref, k_ref, v_ref, ar_ref, o_ref, last) in enumerate(dirs):
        araw = _dot(ar_ref[...], wa_ref[d]) + ba_ref[d:d + 1, :]
        g = (jnp.minimum(araw, 0.0) - jnp.log1p(jnp.exp(-jnp.abs(araw)))) * (LOG2E / 16.0)
        g1 = g.astype(BF16)
        g2 = (g - g1.astype(F32)).astype(BF16)
        ex_mm = _dot(a_ref[d], jnp.concatenate([g1, g2], axis=0))
        bcum = ex_mm[0:GCH]

        def big_level(s):
            parts = []
            for p in range(GCH // (2 * s)):
                lo = 2 * p * s
                ev, od = bcum[lo:lo + s], bcum[lo + s:lo + 2 * s]
                if d == 0:
                    ref = bcum[lo + s:lo + s + 1]
                    parts += [ref - ev, od - ref]
                else:
                    ref = bcum[lo + s - 1:lo + s]
                    parts += [ev - ref, ref - od]
            return jnp.concatenate(parts, axis=0)

        e_cum_all = jnp.exp2(bcum)
        e_rem_all = jnp.exp2((bcum[GCH - 1:GCH] if d == 0 else bcum[0:1]) - bcum)
        e_lv = []
        for lv in range(G_LEVELS):
            s = GCH >> (lv + 1)
            n_big = G_LEVELS - G_SMALL
            ex = big_level(s) if s >= 8 else ex_mm[(1 + lv - n_big) * GCH:(2 + lv - n_big) * GCH]
            e_lv.append(jnp.exp2(ex))

        def zsel(q, k, lv):
            s = GCH >> (lv + 1)
            if s >= 8:
                parts = [(q if (m % 2 == 1) == (d == 0) else k)[m * s:(m + 1) * s]
                         for m in range(GCH // s)]
                return jnp.concatenate(parts, axis=0)
            odd = ((row // s) % 2) == 1
            return jnp.where(odd, q, k) if d == 0 else jnp.where(odd, k, q)

        for h in range(GH):
            sl = slice(h * GDK, (h + 1) * GDK)
            vs = slice(h * GDV, (h + 1) * GDV)
            q = q_ref[:, sl].astype(F32) * (GDK ** -0.5)
            k = k_ref[:, sl].astype(F32)
            v = v_ref[:, vs]
            e_cum = e_cum_all[:, sl]
            e_rem = e_rem_all[:, sl]
            st = s_scr[d, h]
            o = _dot_nt((q * e_cum).astype(BF16), st.astype(BF16))
            att = m_ref[d, G_LEVELS] * _dot_nt(q.astype(BF16), k.astype(BF16))
            for lv in range(G_LEVELS):
                z = (zsel(q, k, lv) * e_lv[lv][:, sl]).astype(BF16)
                att = att + m_ref[d, lv] * _dot_nt(z, z)
            o = o + _dot(att.astype(BF16), v)
            o_ref[:, vs] = o
            s_scr[d, h] = st * e_cum[last:last + 1, :] + _dot_tn(v, (k * e_rem).astype(BF16))


def _gla_scan(p, a, wa2p, ba, slot, amat, masks):
    def rowblk(b, m):
        return jnp.where(m < GC_CH, b * GC_CH + m, RC // GCH + b * (L // GCH) + (m - GC_CH))

    def bwd(c):
        return jnp.where(c < GC_CH, GC_CH - 1 - c, GN_CH + GC_CH - 1 - c)

    hk = GH * GDK
    hv = GH * GDV

    def specs(order):
        return [
            pl.BlockSpec((GCH, hk), lambda b, c: (rowblk(b, order(c)), 0)),
            pl.BlockSpec((GCH, hk), lambda b, c: (rowblk(b, order(c)), 1)),
            pl.BlockSpec((GCH, hv), lambda b, c: (rowblk(b, order(c)), 1)),
            pl.BlockSpec((GCH, 2 * GRANK), lambda b, c: (rowblk(b, order(c)), 0)),
        ]

    in_specs = specs(lambda c: c) + specs(bwd) + [
        pl.BlockSpec((2, 2 * GRANK, hk), lambda b, c: (0, 0, 0)),
        pl.BlockSpec((1, 2, hk), lambda b, c: (slot, 0, 0)),
        pl.BlockSpec((2, G_ROWS, 2 * GCH), lambda b, c: (0, 0, 0)),
        pl.BlockSpec((2, G_LEVELS + 1, GCH, GCH), lambda b, c: (0, 0, 0, 0)),
    ]
    out_specs = [
        pl.BlockSpec((GCH, hv), lambda b, c: (rowblk(b, c), 0)),
        pl.BlockSpec((GCH, hv), lambda b, c: (rowblk(b, bwd(c)), 0)),
    ]

    def body(qf, kf, vf, af, qb, kb, vb, ab, wa_ref, ba_ref, a_ref, m_ref, of_ref, ob_ref, s_scr):
        _gla_body(qf, kf, vf, af, qb, kb, vb, ab, wa_ref, ba_ref.at[0], a_ref, m_ref, of_ref, ob_ref, s_scr)

    return pl.pallas_call(
        body,
        grid=(B, GN_CH),
        in_specs=in_specs,
        out_specs=out_specs,
        out_shape=[jax.ShapeDtypeStruct((R, hv), F32)] * 2,
        scratch_shapes=[pltpu.VMEM((2, GH, GDV, GDK), F32)],
        compiler_params=_cparams(("parallel", "arbitrary")),
        name="gla_scan",
    )(p, p, p, a, p, p, p, a, wa2p, ba, amat, masks)


def _swa_body(sink_ref, q_ref, kc_ref, vc_ref, k0, k1, k2, v0, v1, v2, o_ref):
    j = pl.program_id(1)
    is_ctx = j < LC // QB
    jl = j - LC // QB
    lo = jnp.where(is_ctx, 0, jnp.where(jl >= 1, 0, QB))
    hi = jnp.where(is_ctx, 0, jnp.where(jl <= L // QB - 2, 3 * QB, 2 * QB))
    nk = LC + 3 * QB
    rr = lax.broadcasted_iota(jnp.int32, (QB, nk), 0)
    tt = lax.broadcasted_iota(jnp.int32, (QB, nk), 1) - LC
    dlt = tt - rr
    valid = (tt < 0) | ((dlt >= 0) & (dlt <= 2 * WIN) & (tt >= lo) & (tt < hi))
    bias = jnp.where(valid, 0.0, NEG_INF)

    kcat = jnp.concatenate([kc_ref[...], k0[...], k1[...], k2[...]], axis=0).astype(F32)
    vcat = jnp.concatenate([vc_ref[...], v0[...], v1[...], v2[...]], axis=0).astype(F32)
    lane_q = lax.broadcasted_iota(jnp.int32, (QB, 128), 1)

    def dup(xp, half):
        low = lax.broadcasted_iota(jnp.int32, xp.shape, 1) < HD
        rolled = pltpu.roll(xp, HD, 1)
        out = jnp.where(low, xp, rolled) if half == 0 else jnp.where(low, rolled, xp)
        return out.astype(BF16)

    for kh in range(NKV):
        ps = slice((kh // 2) * 128, (kh // 2 + 1) * 128)
        kd, vd = dup(kcat[:, ps], kh % 2), dup(vcat[:, ps], kh % 2)
        qs = []
        for g in range(GRP):
            h = kh * GRP + g
            blk = q_ref[:, (h // 2) * 128:(h // 2 + 1) * 128].astype(F32)
            keep = (lane_q < HD) if h % 2 == 0 else (lane_q >= HD)
            qs.append(jnp.where(keep, blk, 0.0).astype(BF16))
        qst = jnp.concatenate(qs, axis=0)
        logits = _dot_nt(qst, kd)
        es, invs = [], []
        for g in range(GRP):
            lg = logits[g * QB:(g + 1) * QB] + bias
            s = sink_ref[0, kh * GRP + g]
            m = jnp.maximum(jnp.max(lg, axis=-1, keepdims=True), s)
            e = jnp.exp(lg - m)
            den = jnp.sum(e, axis=-1, keepdims=True) + jnp.exp(s - m)
            es.append(e.astype(BF16))
            invs.append(1.0 / den)
        res = _dot(jnp.concatenate(es, axis=0), vd) * jnp.concatenate(invs, axis=0)
        for m2 in range(GRP // 2):
            even = res[(2 * m2) * QB:(2 * m2 + 1) * QB]
            odd = res[(2 * m2 + 1) * QB:(2 * m2 + 2) * QB]
            col = (kh * (GRP // 2) + m2) * 128
            o_ref[:, col:col + 128] = jnp.where(lane_q < HD, even, odd).astype(o_ref.dtype)


def _swa_attention(p, sink):
    ncb = LC // QB
    nlb = L // QB
    lat0 = RC // QB
    kcol = NQ // NKVD

    def qrow(b, j):
        return jnp.where(j < ncb, b * ncb + j, lat0 + b * nlb + (j - ncb))

    def wrow(off):
        def f(b, j):
            jl = jnp.clip(j - ncb + off, 0, nlb - 1)
            return lat0 + b * nlb + jl
        return f

    in_specs = [
        pl.BlockSpec(memory_space=pltpu.SMEM),
        pl.BlockSpec((QB, NQ), lambda b, j: (qrow(b, j), 0)),
        pl.BlockSpec((LC, NKVD), lambda b, j: (b, kcol)),
        pl.BlockSpec((LC, NKVD), lambda b, j: (b, kcol + 1)),
    ]
    for col in (kcol, kcol + 1):
        for off in (-1, 0, 1):
            in_specs.append(pl.BlockSpec((QB, NKVD), functools.partial(
                lambda b, j, f, cc: (f(b, j), cc), f=wrow(off), cc=col)))
    return pl.pallas_call(
        _swa_body,
        grid=(B, ncb + nlb),
        in_specs=in_specs,
        out_specs=pl.BlockSpec((QB, NQ), lambda b, j: (qrow(b, j), 0)),
        out_shape=jax.ShapeDtypeStruct((R, NQ), BF16),
        compiler_params=_cparams(("parallel", "parallel")),
        name="swa_attn",
    )(sink, p, p, p, p, p, p, p, p, p)


def _gla_prologue(rows, of_ref, ob_ref, og_ref, on_ref):
    o = of_ref[rows, :] + ob_ref[rows, :]
    parts = [_rms(o[:, h * GDV:(h + 1) * GDV], on_ref[0, :, h * GDV:(h + 1) * GDV]) for h in range(GH)]
    return (jnp.concatenate(parts, axis=-1) * _silu(og_ref[rows, :].astype(F32))).astype(BF16)


def _swa_prologue(rows, o_ref):
    return o_ref[rows, :]


def _gmlp_prologue(rows, u_ref, v_ref, lng_ref, lnb_ref, ws_ref, bsb_ref, lhs_scr):
    v = v_ref[rows, :].astype(F32)
    mu = jnp.mean(v, axis=-1, keepdims=True)
    var = jnp.mean(jnp.square(v - mu), axis=-1, keepdims=True)
    vn = ((v - mu) * lax.rsqrt(var + EPS) * lng_ref[0] + lnb_ref[0]).astype(BF16)
    for ch in range(v.shape[0] // MC):
        rs = slice(ch * MC, (ch + 1) * MC)
        rg = slice(rows.start + ch * MC, rows.start + (ch + 1) * MC)
        for g in range(MG):
            cs = slice(g * 128, (g + 1) * 128)
            mixed = _dot(ws_ref[0, g], vn[rs, cs]) + bsb_ref[:, cs]
            lhs_scr[rg, cs] = (u_ref[rg, cs].astype(F32) * mixed).astype(BF16)
    return lhs_scr[rows, :]


OUT_SPLIT = 2


def _out_body(prologue, n_in, dual, *refs):
    in_refs = refs[:n_in]
    rest = list(refs[n_in:])
    wo_ref = rest.pop(0)
    if dual:
        load_x = _two_source_rows(pl.program_id(0) < RC // TM_OUT, rest.pop(0), rest.pop(0))
    else:
        x_ref = rest.pop(0)
        load_x = lambda rows: x_ref[rows, :]
    mod_ref, ng_ref, o_ref = rest[:3]
    piece = o_ref.shape[0] // OUT_SPLIT
    for r in range(OUT_SPLIT):
        rows = slice(r * piece, (r + 1) * piece)
        lhs = prologue(rows, *in_refs, *rest[3:])
        y = _dot(lhs, wo_ref[0])
        o_ref[rows, :] = load_x(rows) + mod_ref[0, 0, 2:3, :] * _rms(y, ng_ref[0, 1:2, :])


def _out_proj(prologue, ins, in_specs, wo, slot, S, mods, norm_g, layer, *, skip_ctx, scratch=(),
              name="out_proj"):
    tm = TM_OUT
    off = RC // tm if skip_ctx else 0
    rows = RX if skip_ctx else R
    specs = [pl.BlockSpec(bs, functools.partial(lambda i, f: f(i + off), f=f)) for bs, f in in_specs]
    dual = isinstance(S, tuple)
    nct = RC // tm
    if dual:
        assert not skip_ctx
        row_specs = [pl.BlockSpec((tm, D), lambda i: (jnp.minimum(i, nct - 1), 0), pipeline_mode=pl.Buffered(1)),
                     pl.BlockSpec((tm, D), lambda i: (jnp.maximum(i - nct, 0), 0))]
    else:
        row_specs = [pl.BlockSpec((tm, D), lambda i: (i + off, 0))]
    specs += [pl.BlockSpec((1, D, D), lambda i: (slot, 0, 0), pipeline_mode=pl.Buffered(1))] + row_specs + [
        pl.BlockSpec((1, 1, 6, D), lambda i: (layer, _mod_row(i + off, tm), 0, 0)),
        pl.BlockSpec((1, 4, D), lambda i: (layer, 0, 0)),
    ]
    return pl.pallas_call(
        functools.partial(_out_body, prologue, len(ins), dual),
        grid=(rows // tm,),
        in_specs=specs,
        out_specs=pl.BlockSpec((tm, D), lambda i: (i, 0)),
        out_shape=jax.ShapeDtypeStruct((rows, D), F32),
        scratch_shapes=list(scratch),
        compiler_params=_cparams(("parallel",)),
        name=name,
    )(*ins, wo, *(S if dual else (S,)), mods, norm_g)


def _ffn_body(nf, x_ref, mod_ref, ng_ref, w1_ref, w3_ref, w2_ref, o_ref, h_scr, rs_scr):
    f = pl.program_id(1)

    @pl.when(f == 0)
    def _():
        _row_rsqrt(lambda rows: x_ref[rows, :], rs_scr)
        gsc = ng_ref[0, 2:3, :] * (1.0 + mod_ref[0, 0, 4:5, :])

        def pro(rows):
            h = x_ref[rows, :] * _lanes(rs_scr[rows, :], D) * gsc + mod_ref[0, 0, 3:4, :]
            h_scr[rows, :] = h.astype(BF16)
        _for_row_chunks(x_ref.shape[0], pro)

    def partial_sum():
        hb = h_scr[...]
        a = _dot(hb, w1_ref[0].astype(BF16))
        b = _dot(hb, w3_ref[0].astype(BF16))
        return _dot((_silu(a) * b).astype(BF16), w2_ref[0].astype(BF16))

    @pl.when(f == 0)
    def _():
        o_ref[...] = partial_sum()

    @pl.when(f > 0)
    def _():
        o_ref[...] += partial_sum()

    @pl.when(f == nf - 1)
    def _():
        _row_rsqrt(lambda rows: o_ref[rows, :], rs_scr)
        gg = mod_ref[0, 0, 5:6, :] * ng_ref[0, 3:4, :]

        def epi(rows):
            o_ref[rows, :] = x_ref[rows, :] + o_ref[rows, :] * _lanes(rs_scr[rows, :], D) * gg
        _for_row_chunks(x_ref.shape[0], epi)


def _ffn(S, mods, norm_g, w1, w3, w2, layer):
    tm, tf = TM_FFN, TF_FFN
    nf = FF // tf
    off = (R - S.shape[0]) // tm
    return pl.pallas_call(
        functools.partial(_ffn_body, nf),
        grid=(S.shape[0] // tm, nf),
        in_specs=[
            pl.BlockSpec((tm, D), lambda i, f: (i, 0)),
            pl.BlockSpec((1, 1, 6, D), lambda i, f: (layer, _mod_row(i + off, tm), 0, 0)),
            pl.BlockSpec((1, 4, D), lambda i, f: (layer, 0, 0)),
            pl.BlockSpec((1, D, tf), lambda i, f: (layer, 0, f)),
            pl.BlockSpec((1, D, tf), lambda i, f: (layer, 0, f)),
            pl.BlockSpec((1, tf, D), lambda i, f: (layer, f, 0)),
        ],
        out_specs=pl.BlockSpec((tm, D), lambda i, f: (i, 0)),
        out_shape=jax.ShapeDtypeStruct(S.shape, F32),
        scratch_shapes=[pltpu.VMEM((tm, D), BF16), pltpu.VMEM((tm, 128), F32)],
        compiler_params=_cparams(("parallel", "arbitrary")),
        name="ffn",
    )(S, mods, norm_g, w1, w3, w2)


def _rope_tables():
    quarter = HD // 4
    inv_freq = 10000.0 ** (-jnp.arange(quarter, dtype=F32) / quarter)
    row = jnp.repeat(jnp.arange(L // 64), 64).astype(F32)
    col = jnp.tile(jnp.arange(64), L // 64).astype(F32)
    ang_r = row[:, None] * inv_freq
    ang_c = col[:, None] * inv_freq
    ang = jnp.concatenate([ang_r, ang_r, ang_c, ang_c], axis=-1)
    reps = SUB_N // HD
    cos, sin = jnp.tile(jnp.cos(ang), (1, reps)), jnp.tile(jnp.sin(ang), (1, reps))
    first = (jnp.arange(SUB_N) % 32) < 16
    return cos, jnp.stack([jnp.where(first, -sin, 0.0), jnp.where(first, 0.0, sin)])


def kernel(x, c, ctx, c_ctx, ada_w, ada_b, norm_g, ffn_w1, ffn_w3, ffn_w2,
           gla_w_in, gla_wa2, gla_ba, gla_onorm_g, gla_wo,
           attn_w_in, attn_sink, attn_wo,
           gmlp_w_in, gmlp_ln_g, gmlp_ln_b, gmlp_ws, gmlp_bs, gmlp_wo):
    S = (ctx.reshape(RC, D), x.reshape(RX, D))
    cvec = jnp.concatenate([c_ctx[None, :], c, jnp.zeros((3, D), F32)], axis=0)
    mods = _modulation(cvec, ada_w, ada_b).reshape(DEPTH, 8, 6, D)
    rowtile = lambda i: (i, 0)
    w1, w3, w2 = ffn_w1, ffn_w3, ffn_w2
    gla_w, gla_wo_b = jnp.swapaxes(gla_w_in, 1, 2), gla_wo.astype(BF16)
    hk, hv = GH * GDK, GH * GDV

    for i in range(DEPTH):
        last = i == DEPTH - 1
        kind, slot = i % 3, i // 3
        if kind == 0:
            p, a = _proj(S, mods, norm_g, i, gla_w, slot, 2 * hk + 2 * hv, w_t=True,
                         small_rows=2 * GRANK, name="gla_proj")
            wa2 = gla_wa2[slot]
            wa2p = jnp.zeros((2, 2 * GRANK, hk), F32)
            wa2p = wa2p.at[0, :GRANK].set(wa2[0]).at[1, GRANK:].set(wa2[1]).astype(BF16)
            amat, masks = _gla_constants()
            o_f, o_b = _gla_scan(p, a, wa2p, gla_ba, slot, amat, masks)
            ins = [o_f, o_b, p, gla_onorm_g.reshape(-1, 1, hv)]
            in_specs = [((TM_OUT, hv), rowtile), ((TM_OUT, hv), rowtile),
                        ((TM_OUT, hv), lambda i: (i, 2)),
                        ((1, 1, hv), functools.partial(lambda i, s: (s, 0, 0), s=slot))]
            S = _out_proj(_gla_prologue, ins, in_specs, gla_wo_b, slot, S, mods, norm_g, i,
                          skip_ctx=last, name="gla_out")
        elif kind == 1:
            cos, sin = _rope_tables()
            p = _proj(S, mods, norm_g, i, attn_w_in, slot, NQ + 2 * NKVD, kind="rope",
                      cos=cos, sin=sin, name="swa_proj")
            o = _swa_attention(p, attn_sink[slot:slot + 1])
            S = _out_proj(_swa_prologue, [o], [((TM_OUT, NQ), rowtile)], attn_wo.astype(BF16), slot,
                          S, mods, norm_g, i, skip_ctx=last, name="swa_out")
        else:
            p = _proj(S, mods, norm_g, i, gmlp_w_in, slot, 2 * D, kind="gelu",
                      name="gmlp_proj")
            bsb = jnp.repeat(gmlp_bs[slot].T, 128, axis=1)
            sel = functools.partial(lambda i, s: (s, 0, 0), s=slot)
            ins = [p, p, gmlp_ln_g.reshape(-1, 1, D), gmlp_ln_b.reshape(-1, 1, D),
                   gmlp_ws.astype(BF16), bsb]
            in_specs = [((TM_OUT, D), rowtile), ((TM_OUT, D), lambda i: (i, 1)),
                        ((1, 1, D), sel), ((1, 1, D), sel),
                        ((1, MG, MC, MC), functools.partial(lambda i, s: (s, 0, 0, 0), s=slot)),
                        ((MC, D), lambda i: (0, 0))]
            S = _out_proj(_gmlp_prologue, ins, in_specs, gmlp_wo.astype(BF16), slot, S, mods, norm_g, i,
                          skip_ctx=last, scratch=[pltpu.VMEM((TM_OUT, D), BF16)], name="gmlp_out")
        S = _ffn(S, mods, norm_g, w1, w3, w2, i)
    return S.reshape(B, L, D)
```

```python
import functools

import numpy as np
import jax
import jax.numpy as jnp
from jax import lax
from jax.experimental import pallas as pl
from jax.experimental.pallas import tpu as pltpu

F32 = jnp.float32
BF16 = jnp.bfloat16

D = 2048
B = 4
L = 2048
LC = 256
DEPTH = 4
FF = 5632
EPS = 1e-6
NEG_INF = -1e30
LOG2E = 1.4426950408889634

RC = B * LC
RX = B * L
R = RC + RX

GH = 4
GDK = 256
GDV = 512
GRANK = 16
GCH = 128
GN_CH = (LC + L) // GCH
GC_CH = LC // GCH
G_LEVELS = 7
G_SMALL = 3
G_ROWS = (1 + G_SMALL) * GCH

HD = 64
NH = 32
NKV = 4
GRP = 8
WIN = 128
QB = 128
NQ = NH * HD
NKVD = NKV * HD

MC = 128
MG = 16

TM_PROJ = 1024
TN_PROJ = (1024, 512)
TM_OUT = 512
TM_FFN = 1024
TF_FFN = 256

VMEM_LIMIT = 56 * 1024 * 1024


def _cparams(sem, vmem=VMEM_LIMIT):
    return pltpu.CompilerParams(dimension_semantics=sem, vmem_limit_bytes=vmem)


def _mod_row(i, tm):
    nct = RC // tm
    return jnp.where(i < nct, 0, 1 + (i - nct) // (L // tm))


def _norm_mod(x, ng, sc, sh):
    ms = jnp.mean(x * x, axis=-1, keepdims=True)
    return (x * lax.rsqrt(ms + EPS) * ng) * (1.0 + sc) + sh


def _rms(y, g):
    ms = jnp.mean(y * y, axis=-1, keepdims=True)
    return y * lax.rsqrt(ms + EPS) * g


ROW_CHUNK = 32


def _for_row_chunks(n_rows, fn, unroll=4):
    def body(r, carry):
        fn(pl.ds(pl.multiple_of(r * ROW_CHUNK, ROW_CHUNK), ROW_CHUNK))
        return carry
    lax.fori_loop(0, n_rows // ROW_CHUNK, body, 0, unroll=unroll)


def _row_rsqrt(load, rs_scr):
    def body(rows):
        x = load(rows)
        ms = jnp.mean(x * x, axis=-1, keepdims=True)
        rs_scr[rows, :] = jnp.broadcast_to(lax.rsqrt(ms + EPS), (ROW_CHUNK, 128))
    _for_row_chunks(rs_scr.shape[0], body, unroll=8)


def _lanes(rs, width):
    return jnp.concatenate([rs] * (width // 128), axis=1)


def _silu(a):
    return a * jax.nn.sigmoid(a)


def _dot(a, b):
    return jnp.dot(a, b, preferred_element_type=F32)


def _dot_nt(a, b):
    return lax.dot_general(a, b, (((1,), (1,)), ((), ())), preferred_element_type=F32)


def _dot_tn(a, b):
    return lax.dot_general(a, b, (((0,), (0,)), ((), ())), preferred_element_type=F32)


def _mod_body(c_ref, w_ref, b_ref, o_ref):
    s = _silu(c_ref[...]).astype(BF16)
    o_ref[0] = _dot(s, w_ref[0].astype(BF16)) + b_ref[0]


def _modulation(cvec, ada_w, ada_b):
    tn = 1024
    n6 = 6 * D
    return pl.pallas_call(
        _mod_body,
        grid=(DEPTH, n6 // tn),
        in_specs=[
            pl.BlockSpec((8, D), lambda l, n: (0, 0)),
            pl.BlockSpec((1, D, tn), lambda l, n: (l, 0, n)),
            pl.BlockSpec((1, 1, tn), lambda l, n: (l, 0, n)),
        ],
        out_specs=pl.BlockSpec((1, 8, tn), lambda l, n: (l, 0, n)),
        out_shape=jax.ShapeDtypeStruct((DEPTH, 8, n6), F32),
        compiler_params=_cparams(("parallel", "parallel")),
        name="adaln_mod",
    )(cvec, ada_w, ada_b.reshape(DEPTH, 1, n6))


SUB_N = 256


def _two_source_rows(is_ctx, c_ref, x_ref):
    return lambda rows: jnp.where(is_ctx, c_ref[rows, :], x_ref[rows, :])


def _proj_body(kind, small_rows, w_t, dual, tm, tn, *refs):
    rest = list(refs)
    if dual:
        load_x = _two_source_rows(pl.program_id(0) < RC // tm, rest.pop(0), rest.pop(0))
    else:
        x_ref = rest.pop(0)
        load_x = lambda rows: x_ref[rows, :]
    mod_ref, ng_ref, w_ref = rest.pop(0), rest.pop(0), rest.pop(0)
    if kind == "rope":
        cos_ref, sin_ref = rest.pop(0), rest.pop(0)
    if small_rows:
        ws_ref = rest.pop(0)
    o_ref = rest.pop(0)
    if small_rows:
        os_ref = rest.pop(0)
    h_scr, rs_scr = rest.pop(0), rest.pop(0)
    i = pl.program_id(0)
    n = pl.program_id(1)

    @pl.when(n == 0)
    def _():
        _row_rsqrt(load_x, rs_scr)
        gsc = ng_ref[0, 0:1, :] * (1.0 + mod_ref[0, 0, 1:2, :])

        def pro(rows):
            h = load_x(rows) * _lanes(rs_scr[rows, :], D) * gsc + mod_ref[0, 0, 0:1, :]
            h_scr[rows, :] = h.astype(BF16)
        _for_row_chunks(tm, pro)
        if small_rows:
            os_ref[...] = _dot_nt(h_scr[...], ws_ref[0].astype(BF16)).astype(os_ref.dtype)

    hb = h_scr[...]
    for c in range(tn // SUB_N):
        cs = slice(c * SUB_N, (c + 1) * SUB_N)
        if w_t:
            acc = _dot_nt(hb, w_ref[0, cs, :].astype(BF16))
        else:
            acc = _dot(hb, w_ref[0, :, cs].astype(BF16))
        if kind == "none":
            o_ref[:, cs] = acc.astype(o_ref.dtype)
        elif kind == "gelu":
            o_ref[:, cs] = jax.nn.gelu(acc, approximate=True).astype(o_ref.dtype)
        else:
            piece = n * (tn // SUB_N) + c
            scale = jnp.where(piece < NQ // SUB_N, HD ** -0.5, 1.0).astype(F32)
            rotate = jnp.logical_and(i >= RC // tm, piece < (NQ + NKVD) // SUB_N)

            roped = (acc * cos_ref[...] + pltpu.roll(acc, SUB_N - 16, 1) * sin_ref[0]
                     + pltpu.roll(acc, 16, 1) * sin_ref[1])
            o_ref[:, cs] = (jnp.where(rotate, roped, acc) * scale).astype(o_ref.dtype)


def _proj(S, mods, norm_g, layer, w, slot, n_cols, *, kind="none", w_t=False, small_rows=0,
          cos=None, sin=None, name="proj"):
    tm = TM_PROJ
    tn = max(t for t in TN_PROJ if n_cols % t == 0)
    nct = RC // tm
    once = pl.Buffered(1)
    dual = isinstance(S, tuple)
    if dual:
        row_specs = [pl.BlockSpec((tm, D), lambda i, n: (jnp.minimum(i, nct - 1), 0), pipeline_mode=once),
                     pl.BlockSpec((tm, D), lambda i, n: (jnp.maximum(i - nct, 0), 0))]
    else:
        row_specs = [pl.BlockSpec((tm, D), lambda i, n: (i, 0))]
    in_specs = row_specs + [
        pl.BlockSpec((1, 1, 6, D), lambda i, n: (layer, _mod_row(i, tm), 0, 0)),
        pl.BlockSpec((1, 4, D), lambda i, n: (layer, 0, 0)),
        (pl.BlockSpec((1, tn, D), lambda i, n: (slot, n, 0)) if w_t
         else pl.BlockSpec((1, D, tn), lambda i, n: (slot, 0, n))),
    ]
    args = (list(S) if dual else [S]) + [mods, norm_g, w]
    if kind == "rope":
        pos = lambda i, n: jnp.where(i < nct, 0, (i - nct) % (L // tm))
        in_specs += [pl.BlockSpec((tm, SUB_N), lambda i, n: (pos(i, n), 0)),
                     pl.BlockSpec((2, tm, SUB_N), lambda i, n: (0, pos(i, n), 0))]
        args += [cos, sin]
    out_specs = [pl.BlockSpec((tm, tn), lambda i, n: (i, n))]
    out_shape = [jax.ShapeDtypeStruct((R, n_cols), BF16)]
    if small_rows:
        assert w_t and n_cols % small_rows == 0
        in_specs.append(pl.BlockSpec((1, small_rows, D), lambda i, n: (slot, n_cols // small_rows, 0)))
        args.append(w)
        out_specs.append(pl.BlockSpec((tm, small_rows), lambda i, n: (i, 0)))
        out_shape.append(jax.ShapeDtypeStruct((R, small_rows), BF16))
    res = pl.pallas_call(
        functools.partial(_proj_body, kind, small_rows, w_t, dual, tm, tn),
        grid=(R // tm, n_cols // tn),
        in_specs=in_specs,
        out_specs=out_specs,
        out_shape=out_shape,
        scratch_shapes=[pltpu.VMEM((tm, D), BF16), pltpu.VMEM((tm, 128), F32)],
        compiler_params=_cparams(("parallel", "arbitrary")),
        name=name,
    )(*args)
    return res if small_rows else res[0]


def _gla_constants():
    c = GCH
    r = np.arange(c)[:, None]
    t = np.arange(c)[None, :]
    blocks = [(t <= r)]
    masks = []
    for lv in range(G_LEVELS):
        s = c >> (lv + 1)
        m = r // s
        odd = (m % 2) == 1
        if s < 8:
            a_odd = (t > s * m) & (t <= r)
            a_even = (t > r) & (t <= s * (m + 1))
            blocks.append(np.where(odd, a_odd, a_even))
        masks.append(odd & ((t // s) == m - 1))
    masks.append(r == t)
    a_f = np.concatenate(blocks, axis=0).astype(np.float32)
    m_f = np.stack(masks).astype(np.float32)
    a_b = np.concatenate([blk[::-1, ::-1] for blk in blocks], axis=0).astype(np.float32)
    m_b = m_f[:, ::-1, ::-1]
    amat = np.stack([a_f, a_b])
    amat = np.concatenate([amat, amat], axis=2)
    return jnp.asarray(amat, BF16), jnp.asarray(np.stack([m_f, m_b]), F32)


def _gla_body(qf, kf, vf, af, qb, kb, vb, ab, wa_ref, ba_ref, a_ref, m_ref, of_ref, ob_ref, s_scr):
    c = pl.program_id(1)

    @pl.when(c == 0)
    def _():
        s_scr[...] = jnp.zeros_like(s_scr)

    row = lax.broadcasted_iota(jnp.int32, (GCH, GDK), 0)
    dirs = ((qf, kf, vf, af, of_ref, GCH - 1), (qb, kb, vb, ab, ob_ref, 0))
    for d, (q_ref, k_ref, v_ref, ar_ref, o_ref, last) in enumerate(dirs):
        araw = _dot(ar_ref[...], wa_ref[d]) + ba_ref[d:d + 1, :]
        g = (jnp.minimum(araw, 0.0) - jnp.log1p(jnp.exp(-jnp.abs(araw)))) * (LOG2E / 16.0)
        g1 = g.astype(BF16)
        g2 = (g - g1.astype(F32)).astype(BF16)
        ex_mm = _dot(a_ref[d], jnp.concatenate([g1, g2], axis=0))
        bcum = ex_mm[0:GCH]

        def big_level(s):
            parts = []
            for p in range(GCH // (2 * s)):
                lo = 2 * p * s
                ev, od = bcum[lo:lo + s], bcum[lo + s:lo + 2 * s]
                if d == 0:
                    ref = bcum[lo + s:lo + s + 1]
                    parts += [ref - ev, od - ref]
                else:
                    ref = bcum[lo + s - 1:lo + s]
                    parts += [ev - ref, ref - od]
            return jnp.concatenate(parts, axis=0)

        e_cum_all = jnp.exp2(bcum)
        e_rem_all = jnp.exp2((bcum[GCH - 1:GCH] if d == 0 else bcum[0:1]) - bcum)
        e_lv = []
        for lv in range(G_LEVELS):
            s = GCH >> (lv + 1)
            n_big = G_LEVELS - G_SMALL
            ex = big_level(s) if s >= 8 else ex_mm[(1 + lv - n_big) * GCH:(2 + lv - n_big) * GCH]
            e_lv.append(jnp.exp2(ex))

        def zsel(q, k, lv):
            s = GCH >> (lv + 1)
            if s >= 8:
                parts = [(q if (m % 2 == 1) == (d == 0) else k)[m * s:(m + 1) * s]
                         for m in range(GCH // s)]
                return jnp.concatenate(parts, axis=0)
            odd = ((row // s) % 2) == 1
            return jnp.where(odd, q, k) if d == 0 else jnp.where(odd, k, q)

        for h in range(GH):
            sl = slice(h * GDK, (h + 1) * GDK)
            vs = slice(h * GDV, (h + 1) * GDV)
            q = q_ref[:, sl].astype(F32) * (GDK ** -0.5)
            k = k_ref[:, sl].astype(F32)
            v = v_ref[:, vs]
            e_cum = e_cum_all[:, sl]
            e_rem = e_rem_all[:, sl]
            st = s_scr[d, h]
            o = _dot_nt((q * e_cum).astype(BF16), st.astype(BF16))
            att = m_ref[d, G_LEVELS] * _dot_nt(q.astype(BF16), k.astype(BF16))
            for lv in range(G_LEVELS):
                z = (zsel(q, k, lv) * e_lv[lv][:, sl]).astype(BF16)
                att = att + m_ref[d, lv] * _dot_nt(z, z)
            o = o + _dot(att.astype(BF16), v)
            o_ref[:, vs] = o.astype(o_ref.dtype)
            s_scr[d, h] = st * e_cum[last:last + 1, :] + _dot_tn(v, (k * e_rem).astype(BF16))


def _gla_scan(p, a, wa2p, ba, slot, amat, masks):
    def rowblk(b, m):
        return jnp.where(m < GC_CH, b * GC_CH + m, RC // GCH + b * (L // GCH) + (m - GC_CH))

    def bwd(c):
        return jnp.where(c < GC_CH, GC_CH - 1 - c, GN_CH + GC_CH - 1 - c)

    hk = GH * GDK
    hv = GH * GDV

    def specs(order):
        return [
            pl.BlockSpec((GCH, hk), lambda b, c: (rowblk(b, order(c)), 0)),
            pl.BlockSpec((GCH, hk), lambda b, c: (rowblk(b, order(c)), 1)),
            pl.BlockSpec((GCH, hv), lambda b, c: (rowblk(b, order(c)), 1)),
            pl.BlockSpec((GCH, 2 * GRANK), lambda b, c: (rowblk(b, order(c)), 0)),
        ]

    in_specs = specs(lambda c: c) + specs(bwd) + [
        pl.BlockSpec((2, 2 * GRANK, hk), lambda b, c: (0, 0, 0)),
        pl.BlockSpec((1, 2, hk), lambda b, c: (slot, 0, 0)),
        pl.BlockSpec((2, G_ROWS, 2 * GCH), lambda b, c: (0, 0, 0)),
        pl.BlockSpec((2, G_LEVELS + 1, GCH, GCH), lambda b, c: (0, 0, 0, 0)),
    ]
    out_specs = [
        pl.BlockSpec((GCH, hv), lambda b, c: (rowblk(b, c), 0)),
        pl.BlockSpec((GCH, hv), lambda b, c: (rowblk(b, bwd(c)), 0)),
    ]

    def body(qf, kf, vf, af, qb, kb, vb, ab, wa_ref, ba_ref, a_ref, m_ref, of_ref, ob_ref, s_scr):
        _gla_body(qf, kf, vf, af, qb, kb, vb, ab, wa_ref, ba_ref.at[0], a_ref, m_ref, of_ref, ob_ref, s_scr)

    return pl.pallas_call(
        body,
        grid=(B, GN_CH),
        in_specs=in_specs,
        out_specs=out_specs,
        out_shape=[jax.ShapeDtypeStruct((R, hv), BF16)] * 2,
        scratch_shapes=[pltpu.VMEM((2, GH, GDV, GDK), F32)],
        compiler_params=_cparams(("parallel", "arbitrary")),
        name="gla_scan",
    )(p, p, p, a, p, p, p, a, wa2p, ba, amat, masks)


def _swa_body(sink_ref, q_ref, kc_ref, vc_ref, k0, k1, k2, v0, v1, v2, o_ref):
    j = pl.program_id(1)
    is_ctx = j < LC // QB
    jl = j - LC // QB
    lo = jnp.where(is_ctx, 0, jnp.where(jl >= 1, 0, QB))
    hi = jnp.where(is_ctx, 0, jnp.where(jl <= L // QB - 2, 3 * QB, 2 * QB))
    nk = LC + 3 * QB
    rr = lax.broadcasted_iota(jnp.int32, (QB, nk), 0)
    tt = lax.broadcasted_iota(jnp.int32, (QB, nk), 1) - LC
    dlt = tt - rr
    valid = (tt < 0) | ((dlt >= 0) & (dlt <= 2 * WIN) & (tt >= lo) & (tt < hi))
    bias = jnp.where(valid, 0.0, NEG_INF)

    kcat = jnp.concatenate([kc_ref[...], k0[...], k1[...], k2[...]], axis=0).astype(F32)
    vcat = jnp.concatenate([vc_ref[...], v0[...], v1[...], v2[...]], axis=0).astype(F32)
    lane_q = lax.broadcasted_iota(jnp.int32, (QB, 128), 1)

    def dup(xp, half):
        low = lax.broadcasted_iota(jnp.int32, xp.shape, 1) < HD
        rolled = pltpu.roll(xp, HD, 1)
        out = jnp.where(low, xp, rolled) if half == 0 else jnp.where(low, rolled, xp)
        return out.astype(BF16)

    for kh in range(NKV):
        ps = slice((kh // 2) * 128, (kh // 2 + 1) * 128)
        kd, vd = dup(kcat[:, ps], kh % 2), dup(vcat[:, ps], kh % 2)
        qs = []
        for g in range(GRP):
            h = kh * GRP + g
            blk = q_ref[:, (h // 2) * 128:(h // 2 + 1) * 128].astype(F32)
            keep = (lane_q < HD) if h % 2 == 0 else (lane_q >= HD)
            qs.append(jnp.where(keep, blk, 0.0).astype(BF16))
        qst = jnp.concatenate(qs, axis=0)
        logits = _dot_nt(qst, kd)
        es, invs = [], []
        for g in range(GRP):
            lg = logits[g * QB:(g + 1) * QB] + bias
            s = sink_ref[0, kh * GRP + g]
            m = jnp.maximum(jnp.max(lg, axis=-1, keepdims=True), s)
            e = jnp.exp(lg - m)
            den = jnp.sum(e, axis=-1, keepdims=True) + jnp.exp(s - m)
            es.append(e.astype(BF16))
            invs.append(1.0 / den)
        res = _dot(jnp.concatenate(es, axis=0), vd) * jnp.concatenate(invs, axis=0)
        for m2 in range(GRP // 2):
            even = res[(2 * m2) * QB:(2 * m2 + 1) * QB]
            odd = res[(2 * m2 + 1) * QB:(2 * m2 + 2) * QB]
            col = (kh * (GRP // 2) + m2) * 128
            o_ref[:, col:col + 128] = jnp.where(lane_q < HD, even, odd).astype(o_ref.dtype)


def _swa_attention(p, sink):
    ncb = LC // QB
    nlb = L // QB
    lat0 = RC // QB
    kcol = NQ // NKVD

    def qrow(b, j):
        return jnp.where(j < ncb, b * ncb + j, lat0 + b * nlb + (j - ncb))

    def wrow(off):
        def f(b, j):
            jl = jnp.clip(j - ncb + off, 0, nlb - 1)
            return lat0 + b * nlb + jl
        return f

    in_specs = [
        pl.BlockSpec(memory_space=pltpu.SMEM),
        pl.BlockSpec((QB, NQ), lambda b, j: (qrow(b, j), 0)),
        pl.BlockSpec((LC, NKVD), lambda b, j: (b, kcol)),
        pl.BlockSpec((LC, NKVD), lambda b, j: (b, kcol + 1)),
    ]
    for col in (kcol, kcol + 1):
        for off in (-1, 0, 1):
            in_specs.append(pl.BlockSpec((QB, NKVD), functools.partial(
                lambda b, j, f, cc: (f(b, j), cc), f=wrow(off), cc=col)))
    return pl.pallas_call(
        _swa_body,
        grid=(B, ncb + nlb),
        in_specs=in_specs,
        out_specs=pl.BlockSpec((QB, NQ), lambda b, j: (qrow(b, j), 0)),
        out_shape=jax.ShapeDtypeStruct((R, NQ), BF16),
        compiler_params=_cparams(("parallel", "parallel")),
        name="swa_attn",
    )(sink, p, p, p, p, p, p, p, p, p)


def _gla_prologue(rows, of_ref, ob_ref, og_ref, on_ref):
    o = of_ref[rows, :].astype(F32) + ob_ref[rows, :].astype(F32)
    parts = [_rms(o[:, h * GDV:(h + 1) * GDV], on_ref[0, :, h * GDV:(h + 1) * GDV]) for h in range(GH)]
    return (jnp.concatenate(parts, axis=-1) * _silu(og_ref[rows, :].astype(F32))).astype(BF16)


def _swa_prologue(rows, o_ref):
    return o_ref[rows, :]


def _gmlp_prologue(rows, u_ref, v_ref, lng_ref, lnb_ref, ws_ref, bsb_ref, lhs_scr):
    v = v_ref[rows, :].astype(F32)
    mu = jnp.mean(v, axis=-1, keepdims=True)
    var = jnp.mean(jnp.square(v - mu), axis=-1, keepdims=True)
    vn = ((v - mu) * lax.rsqrt(var + EPS) * lng_ref[0] + lnb_ref[0]).astype(BF16)
    for ch in range(v.shape[0] // MC):
        rs = slice(ch * MC, (ch + 1) * MC)
        rg = slice(rows.start + ch * MC, rows.start + (ch + 1) * MC)
        for g in range(MG):
            cs = slice(g * 128, (g + 1) * 128)
            mixed = _dot(ws_ref[0, g], vn[rs, cs]) + bsb_ref[:, cs]
            lhs_scr[rg, cs] = (u_ref[rg, cs].astype(F32) * mixed).astype(BF16)
    return lhs_scr[rows, :]


OUT_SPLIT = 2


def _out_body(prologue, n_in, dual, *refs):
    in_refs = refs[:n_in]
    rest = list(refs[n_in:])
    wo_ref = rest.pop(0)
    if dual:
        load_x = _two_source_rows(pl.program_id(0) < RC // TM_OUT, rest.pop(0), rest.pop(0))
    else:
        x_ref = rest.pop(0)
        load_x = lambda rows: x_ref[rows, :]
    mod_ref, ng_ref, o_ref = rest[:3]
    piece = o_ref.shape[0] // OUT_SPLIT
    for r in range(OUT_SPLIT):
        rows = slice(r * piece, (r + 1) * piece)
        y = _dot(prologue(rows, *in_refs, *rest[3:]), wo_ref[0])
        o_ref[rows, :] = load_x(rows) + mod_ref[0, 0, 2:3, :] * _rms(y, ng_ref[0, 1:2, :])


def _out_proj(prologue, ins, in_specs, wo, slot, S, mods, norm_g, layer, *, skip_ctx, scratch=(),
              name="out_proj"):
    tm = TM_OUT
    off = RC // tm if skip_ctx else 0
    rows = RX if skip_ctx else R
    specs = [pl.BlockSpec(bs, functools.partial(lambda i, f: f(i + off), f=f)) for bs, f in in_specs]
    dual = isinstance(S, tuple)
    nct = RC // tm
    if dual:
        assert not skip_ctx
        row_specs = [pl.BlockSpec((tm, D), lambda i: (jnp.minimum(i, nct - 1), 0), pipeline_mode=pl.Buffered(1)),
                     pl.BlockSpec((tm, D), lambda i: (jnp.maximum(i - nct, 0), 0))]
    else:
        row_specs = [pl.BlockSpec((tm, D), lambda i: (i + off, 0))]
    specs += [pl.BlockSpec((1, D, D), lambda i: (slot, 0, 0), pipeline_mode=pl.Buffered(1))] + row_specs + [
        pl.BlockSpec((1, 1, 6, D), lambda i: (layer, _mod_row(i + off, tm), 0, 0)),
        pl.BlockSpec((1, 4, D), lambda i: (layer, 0, 0)),
    ]
    return pl.pallas_call(
        functools.partial(_out_body, prologue, len(ins), dual),
        grid=(rows // tm,),
        in_specs=specs,
        out_specs=pl.BlockSpec((tm, D), lambda i: (i, 0)),
        out_shape=jax.ShapeDtypeStruct((rows, D), F32),
        scratch_shapes=list(scratch),
        compiler_params=_cparams(("parallel",)),
        name=name,
    )(*ins, wo, *(S if dual else (S,)), mods, norm_g)


def _ffn_body(nf, x_ref, mod_ref, ng_ref, w1_ref, w3_ref, w2_ref, o_ref, h_scr, rs_scr):
    f = pl.program_id(1)

    @pl.when(f == 0)
    def _():
        _row_rsqrt(lambda rows: x_ref[rows, :], rs_scr)
        gsc = ng_ref[0, 2:3, :] * (1.0 + mod_ref[0, 0, 4:5, :])

        def pro(rows):
            h = x_ref[rows, :] * _lanes(rs_scr[rows, :], D) * gsc + mod_ref[0, 0, 3:4, :]
            h_scr[rows, :] = h.astype(BF16)
        _for_row_chunks(x_ref.shape[0], pro)

    def partial_sum():
        hb = h_scr[...]
        a = _dot(hb, w1_ref[0].astype(BF16))
        b = _dot(hb, w3_ref[0].astype(BF16))
        return _dot((_silu(a) * b).astype(BF16), w2_ref[0].astype(BF16))

    @pl.when(f == 0)
    def _():
        o_ref[...] = partial_sum()

    @pl.when(f > 0)
    def _():
        o_ref[...] += partial_sum()

    @pl.when(f == nf - 1)
    def _():
        _row_rsqrt(lambda rows: o_ref[rows, :], rs_scr)
        gg = mod_ref[0, 0, 5:6, :] * ng_ref[0, 3:4, :]

        def epi(rows):
            o_ref[rows, :] = x_ref[rows, :] + o_ref[rows, :] * _lanes(rs_scr[rows, :], D) * gg
        _for_row_chunks(x_ref.shape[0], epi)


def _ffn(S, mods, norm_g, w1, w3, w2, layer):
    tm, tf = TM_FFN, TF_FFN
    nf = FF // tf
    off = (R - S.shape[0]) // tm
    return pl.pallas_call(
        functools.partial(_ffn_body, nf),
        grid=(S.shape[0] // tm, nf),
        in_specs=[
            pl.BlockSpec((tm, D), lambda i, f: (i, 0)),
            pl.BlockSpec((1, 1, 6, D), lambda i, f: (layer, _mod_row(i + off, tm), 0, 0)),
            pl.BlockSpec((1, 4, D), lambda i, f: (layer, 0, 0)),
            pl.BlockSpec((1, D, tf), lambda i, f: (layer, 0, f)),
            pl.BlockSpec((1, D, tf), lambda i, f: (layer, 0, f)),
            pl.BlockSpec((1, tf, D), lambda i, f: (layer, f, 0)),
        ],
        out_specs=pl.BlockSpec((tm, D), lambda i, f: (i, 0)),
        out_shape=jax.ShapeDtypeStruct(S.shape, F32),
        scratch_shapes=[pltpu.VMEM((tm, D), BF16), pltpu.VMEM((tm, 128), F32)],
        compiler_params=_cparams(("parallel", "arbitrary")),
        name="ffn",
    )(S, mods, norm_g, w1, w3, w2)


def _rope_tables():
    quarter = HD // 4
    inv_freq = 10000.0 ** (-jnp.arange(quarter, dtype=F32) / quarter)
    row = jnp.repeat(jnp.arange(L // 64), 64).astype(F32)
    col = jnp.tile(jnp.arange(64), L // 64).astype(F32)
    ang_r = row[:, None] * inv_freq
    ang_c = col[:, None] * inv_freq
    ang = jnp.concatenate([ang_r, ang_r, ang_c, ang_c], axis=-1)
    reps = SUB_N // HD
    cos, sin = jnp.tile(jnp.cos(ang), (1, reps)), jnp.tile(jnp.sin(ang), (1, reps))
    first = (jnp.arange(SUB_N) % 32) < 16
    return cos, jnp.stack([jnp.where(first, -sin, 0.0), jnp.where(first, 0.0, sin)])


def kernel(x, c, ctx, c_ctx, ada_w, ada_b, norm_g, ffn_w1, ffn_w3, ffn_w2,
           gla_w_in, gla_wa2, gla_ba, gla_onorm_g, gla_wo,
           attn_w_in, attn_sink, attn_wo,
           gmlp_w_in, gmlp_ln_g, gmlp_ln_b, gmlp_ws, gmlp_bs, gmlp_wo):
    S = (ctx.reshape(RC, D), x.reshape(RX, D))
    cvec = jnp.concatenate([c_ctx[None, :], c, jnp.zeros((3, D), F32)], axis=0)
    mods = _modulation(cvec, ada_w, ada_b).reshape(DEPTH, 8, 6, D)
    rowtile = lambda i: (i, 0)
    w1, w3, w2 = ffn_w1, ffn_w3, ffn_w2
    gla_w, gla_wo_b = jnp.swapaxes(gla_w_in, 1, 2), gla_wo.astype(BF16)
    hk, hv = GH * GDK, GH * GDV

    for i in range(DEPTH):
        last = i == DEPTH - 1
        kind, slot = i % 3, i // 3
        if kind == 0:
            p, a = _proj(S, mods, norm_g, i, gla_w, slot, 2 * hk + 2 * hv, w_t=True,
                         small_rows=2 * GRANK, name="gla_proj")
            wa2 = gla_wa2[slot]
            wa2p = jnp.zeros((2, 2 * GRANK, hk), F32)
            wa2p = wa2p.at[0, :GRANK].set(wa2[0]).at[1, GRANK:].set(wa2[1]).astype(BF16)
            amat, masks = _gla_constants()
            o_f, o_b = _gla_scan(p, a, wa2p, gla_ba, slot, amat, masks)
            ins = [o_f, o_b, p, gla_onorm_g.reshape(-1, 1, hv)]
            in_specs = [((TM_OUT, hv), rowtile), ((TM_OUT, hv), rowtile),
                        ((TM_OUT, hv), lambda i: (i, 2)),
                        ((1, 1, hv), functools.partial(lambda i, s: (s, 0, 0), s=slot))]
            S = _out_proj(_gla_prologue, ins, in_specs, gla_wo_b, slot, S, mods, norm_g, i,
                          skip_ctx=last, name="gla_out")
        elif kind == 1:
            cos, sin = _rope_tables()
            p = _proj(S, mods, norm_g, i, attn_w_in, slot, NQ + 2 * NKVD, kind="rope",
                      cos=cos, sin=sin, name="swa_proj")
            o = _swa_attention(p, attn_sink[slot:slot + 1])
            S = _out_proj(_swa_prologue, [o], [((TM_OUT, NQ), rowtile)], attn_wo.astype(BF16), slot,
                          S, mods, norm_g, i, skip_ctx=last, name="swa_out")
        else:
            p = _proj(S, mods, norm_g, i, gmlp_w_in, slot, 2 * D, kind="gelu",
                      name="gmlp_proj")
            bsb = jnp.repeat(gmlp_bs[slot].T, 128, axis=1)
            sel = functools.partial(lambda i, s: (s, 0, 0), s=slot)
            ins = [p, p, gmlp_ln_g.reshape(-1, 1, D), gmlp_ln_b.reshape(-1, 1, D),
                   gmlp_ws.astype(BF16), bsb]
            in_specs = [((TM_OUT, D), rowtile), ((TM_OUT, D), lambda i: (i, 1)),
                        ((1, 1, D), sel), ((1, 1, D), sel),
                        ((1, MG, MC, MC), functools.partial(lambda i, s: (s, 0, 0, 0), s=slot)),
                        ((MC, D), lambda i: (0, 0))]
            S = _out_proj(_gmlp_prologue, ins, in_specs, gmlp_wo.astype(BF16), slot, S, mods, norm_g, i,
                          skip_ctx=last, scratch=[pltpu.VMEM((TM_OUT, D), BF16)], name="gmlp_out")
        S = _ffn(S, mods, norm_g, w1, w3, w2, i)
    return S.reshape(B, L, D)
```

```python
import functools

import numpy as np
import jax
import jax.numpy as jnp
from jax import lax
from jax.experimental import pallas as pl
from jax.experimental.pallas import tpu as pltpu

F32 = jnp.float32
BF16 = jnp.bfloat16

D = 2048
B = 4
L = 2048
LC = 256
DEPTH = 4
FF = 5632
EPS = 1e-6
NEG_INF = -1e30
LOG2E = 1.4426950408889634

RC = B * LC
RX = B * L
R = RC + RX

GH = 4
GDK = 256
GDV = 512
GRANK = 16
GCH = 128
GN_CH = (LC + L) // GCH
GC_CH = LC // GCH
G_LEVELS = 7
G_SMALL = 3
G_ROWS = (1 + G_SMALL) * GCH

HD = 64
NH = 32
NKV = 4
GRP = 8
WIN = 128
QB = 128
NQ = NH * HD
NKVD = NKV * HD

MC = 128
MG = 16

TM_PROJ = 1024
TN_PROJ = (1024, 512)
TM_OUT = 512
TM_FFN = 1024
TF_FFN = 256

VMEM_LIMIT = 56 * 1024 * 1024


def _cparams(sem, vmem=VMEM_LIMIT):
    return pltpu.CompilerParams(dimension_semantics=sem, vmem_limit_bytes=vmem)


def _mod_row(i, tm):
    nct = RC // tm
    return jnp.where(i < nct, 0, 1 + (i - nct) // (L // tm))


def _norm_mod(x, ng, sc, sh):
    ms = jnp.mean(x * x, axis=-1, keepdims=True)
    return (x * lax.rsqrt(ms + EPS) * ng) * (1.0 + sc) + sh


def _rms(y, g):
    ms = jnp.mean(y * y, axis=-1, keepdims=True)
    return y * lax.rsqrt(ms + EPS) * g


ROW_CHUNK = 32


def _for_row_chunks(n_rows, fn, unroll=4):
    def body(r, carry):
        fn(pl.ds(pl.multiple_of(r * ROW_CHUNK, ROW_CHUNK), ROW_CHUNK))
        return carry
    lax.fori_loop(0, n_rows // ROW_CHUNK, body, 0, unroll=unroll)


def _row_rsqrt(load, rs_scr):
    def body(rows):
        x = load(rows)
        ms = jnp.mean(x * x, axis=-1, keepdims=True)
        rs_scr[rows, :] = jnp.broadcast_to(lax.rsqrt(ms + EPS), (ROW_CHUNK, 128))
    _for_row_chunks(rs_scr.shape[0], body, unroll=8)


def _lanes(rs, width):
    return jnp.concatenate([rs] * (width // 128), axis=1)


def _silu(a):
    return a * jax.nn.sigmoid(a)


def _dot(a, b):
    return jnp.dot(a, b, preferred_element_type=F32)


def _dot_nt(a, b):
    return lax.dot_general(a, b, (((1,), (1,)), ((), ())), preferred_element_type=F32)


def _dot_tn(a, b):
    return lax.dot_general(a, b, (((0,), (0,)), ((), ())), preferred_element_type=F32)


def _mod_body(c_ref, w_ref, b_ref, o_ref):
    s = _silu(c_ref[...]).astype(BF16)
    o_ref[0] = _dot(s, w_ref[0].astype(BF16)) + b_ref[0]


def _modulation(cvec, ada_w, ada_b):
    tn = 1024
    n6 = 6 * D
    return pl.pallas_call(
        _mod_body,
        grid=(DEPTH, n6 // tn),
        in_specs=[
            pl.BlockSpec((8, D), lambda l, n: (0, 0)),
            pl.BlockSpec((1, D, tn), lambda l, n: (l, 0, n)),
            pl.BlockSpec((1, 1, tn), lambda l, n: (l, 0, n)),
        ],
        out_specs=pl.BlockSpec((1, 8, tn), lambda l, n: (l, 0, n)),
        out_shape=jax.ShapeDtypeStruct((DEPTH, 8, n6), F32),
        compiler_params=_cparams(("parallel", "parallel")),
        name="adaln_mod",
    )(cvec, ada_w, ada_b.reshape(DEPTH, 1, n6))


SUB_N = 256
SUB_M = 256


def _two_source_rows(is_ctx, c_ref, x_ref):
    return lambda rows: jnp.where(is_ctx, c_ref[rows, :], x_ref[rows, :])


def _proj_body(kind, small_rows, w_t, dual, tm, tn, *refs):
    rest = list(refs)
    if dual:
        load_x = _two_source_rows(pl.program_id(0) < RC // tm, rest.pop(0), rest.pop(0))
    else:
        x_ref = rest.pop(0)
        load_x = lambda rows: x_ref[rows, :]
    mod_ref, ng_ref, w_ref = rest.pop(0), rest.pop(0), rest.pop(0)
    if kind == "rope":
        cos_ref, sin_ref = rest.pop(0), rest.pop(0)
    if small_rows:
        ws_ref = rest.pop(0)
    o_ref = rest.pop(0)
    if small_rows:
        os_ref = rest.pop(0)
    h_scr, rs_scr = rest.pop(0), rest.pop(0)
    i = pl.program_id(0)
    n = pl.program_id(1)

    @pl.when(n == 0)
    def _():
        _row_rsqrt(load_x, rs_scr)
        gsc = ng_ref[0, 0:1, :] * (1.0 + mod_ref[0, 0, 1:2, :])

        def pro(rows):
            h = load_x(rows) * _lanes(rs_scr[rows, :], D) * gsc + mod_ref[0, 0, 0:1, :]
            h_scr[rows, :] = h.astype(BF16)
        _for_row_chunks(tm, pro)
        if small_rows:
            os_ref[...] = _dot_nt(h_scr[...], ws_ref[0].astype(BF16)).astype(os_ref.dtype)

    sub_m = SUB_M if kind == "rope" else tm
    for c, r in [(c, r) for c in range(tn // SUB_N) for r in range(tm // sub_m)]:
        cs = slice(c * SUB_N, (c + 1) * SUB_N)
        rs = slice(r * sub_m, (r + 1) * sub_m)
        hb = h_scr[rs, :]
        if w_t:
            acc = _dot_nt(hb, w_ref[0, cs, :].astype(BF16))
        else:
            acc = _dot(hb, w_ref[0, :, cs].astype(BF16))
        if kind == "none":
            o_ref[rs, cs] = acc.astype(o_ref.dtype)
        elif kind == "gelu":
            o_ref[rs, cs] = jax.nn.gelu(acc, approximate=True).astype(o_ref.dtype)
        else:
            piece = n * (tn // SUB_N) + c
            scale = jnp.where(piece < NQ // SUB_N, HD ** -0.5, 1.0).astype(F32)
            rotate = jnp.logical_and(i >= RC // tm, piece < (NQ + NKVD) // SUB_N)

            roped = (acc * cos_ref[rs, :] + pltpu.roll(acc, SUB_N - 16, 1) * sin_ref[0, rs, :]
                     + pltpu.roll(acc, 16, 1) * sin_ref[1, rs, :])
            o_ref[rs, cs] = (jnp.where(rotate, roped, acc) * scale).astype(o_ref.dtype)


def _proj(S, mods, norm_g, layer, w, slot, n_cols, *, kind="none", w_t=False, small_rows=0,
          cos=None, sin=None, name="proj"):
    tm = TM_PROJ
    tn = max(t for t in TN_PROJ if n_cols % t == 0)
    nct = RC // tm
    once = pl.Buffered(1)
    dual = isinstance(S, tuple)
    if dual:
        row_specs = [pl.BlockSpec((tm, D), lambda i, n: (jnp.minimum(i, nct - 1), 0), pipeline_mode=once),
                     pl.BlockSpec((tm, D), lambda i, n: (jnp.maximum(i - nct, 0), 0))]
    else:
        row_specs = [pl.BlockSpec((tm, D), lambda i, n: (i, 0))]
    in_specs = row_specs + [
        pl.BlockSpec((1, 1, 6, D), lambda i, n: (layer, _mod_row(i, tm), 0, 0)),
        pl.BlockSpec((1, 4, D), lambda i, n: (layer, 0, 0)),
        (pl.BlockSpec((1, tn, D), lambda i, n: (slot, n, 0)) if w_t
         else pl.BlockSpec((1, D, tn), lambda i, n: (slot, 0, n))),
    ]
    args = (list(S) if dual else [S]) + [mods, norm_g, w]
    if kind == "rope":
        pos = lambda i, n: jnp.where(i < nct, 0, (i - nct) % (L // tm))
        in_specs += [pl.BlockSpec((tm, SUB_N), lambda i, n: (pos(i, n), 0)),
                     pl.BlockSpec((2, tm, SUB_N), lambda i, n: (0, pos(i, n), 0))]
        args += [cos, sin]
    out_specs = [pl.BlockSpec((tm, tn), lambda i, n: (i, n))]
    out_shape = [jax.ShapeDtypeStruct((R, n_cols), BF16)]
    if small_rows:
        assert w_t and n_cols % small_rows == 0
        in_specs.append(pl.BlockSpec((1, small_rows, D), lambda i, n: (slot, n_cols // small_rows, 0)))
        args.append(w)
        out_specs.append(pl.BlockSpec((tm, small_rows), lambda i, n: (i, 0)))
        out_shape.append(jax.ShapeDtypeStruct((R, small_rows), BF16))
    res = pl.pallas_call(
        functools.partial(_proj_body, kind, small_rows, w_t, dual, tm, tn),
        grid=(R // tm, n_cols // tn),
        in_specs=in_specs,
        out_specs=out_specs,
        out_shape=out_shape,
        scratch_shapes=[pltpu.VMEM((tm, D), BF16), pltpu.VMEM((tm, 128), F32)],
        compiler_params=_cparams(("parallel", "arbitrary")),
        name=name,
    )(*args)
    return res if small_rows else res[0]


def _gla_constants():
    c = GCH
    r = np.arange(c)[:, None]
    t = np.arange(c)[None, :]
    blocks = [(t <= r)]
    masks = []
    for lv in range(G_LEVELS):
        s = c >> (lv + 1)
        m = r // s
        odd = (m % 2) == 1
        if s < 8:
            a_odd = (t > s * m) & (t <= r)
            a_even = (t > r) & (t <= s * (m + 1))
            blocks.append(np.where(odd, a_odd, a_even))
        masks.append(odd & ((t // s) == m - 1))
    masks.append(r == t)
    a_f = np.concatenate(blocks, axis=0).astype(np.float32)
    m_f = np.stack(masks).astype(np.float32)
    a_b = np.concatenate([blk[::-1, ::-1] for blk in blocks], axis=0).astype(np.float32)
    m_b = m_f[:, ::-1, ::-1]
    amat = np.stack([a_f, a_b])
    amat = np.concatenate([amat, amat], axis=2)
    return jnp.asarray(amat, BF16), jnp.asarray(np.stack([m_f, m_b]), F32)


def _gla_body(qf, kf, vf, af, qb, kb, vb, ab, wa_ref, ba_ref, a_ref, m_ref, of_ref, ob_ref, s_scr):
    c = pl.program_id(1)

    @pl.when(c == 0)
    def _():
        s_scr[...] = jnp.zeros_like(s_scr)

    row = lax.broadcasted_iota(jnp.int32, (GCH, GDK), 0)
    dirs = ((qf, kf, vf, af, of_ref, GCH - 1), (qb, kb, vb, ab, ob_ref, 0))
    for d, (q_ref, k_ref, v_ref, ar_ref, o_ref, last) in enumerate(dirs):
        araw = _dot(ar_ref[...], wa_ref[d]) + ba_ref[d:d + 1, :]
        g = (jnp.minimum(araw, 0.0) - jnp.log1p(jnp.exp(-jnp.abs(araw)))) * (LOG2E / 16.0)
        g1 = g.astype(BF16)
        g2 = (g - g1.astype(F32)).astype(BF16)
        ex_mm = _dot(a_ref[d], jnp.concatenate([g1, g2], axis=0))
        def level_exponent(bcum, ex_h, lv):
            s = GCH >> (lv + 1)
            if s < 8:
                n_big = G_LEVELS - G_SMALL
                return ex_h[(1 + lv - n_big) * GCH:(2 + lv - n_big) * GCH]
            parts = []
            for p in range(GCH // (2 * s)):
                lo = 2 * p * s
                ev, od = bcum[lo:lo + s], bcum[lo + s:lo + 2 * s]
                if d == 0:
                    ref = bcum[lo + s:lo + s + 1]
                    parts += [ref - ev, od - ref]
                else:
                    ref = bcum[lo + s - 1:lo + s]
                    parts += [ev - ref, ref - od]
            return jnp.concatenate(parts, axis=0)

        def zsel(q, k, lv):
            s = GCH >> (lv + 1)
            if s >= 8:
                parts = [(q if (m % 2 == 1) == (d == 0) else k)[m * s:(m + 1) * s]
                         for m in range(GCH // s)]
                return jnp.concatenate(parts, axis=0)
            odd = ((row // s) % 2) == 1
            return jnp.where(odd, q, k) if d == 0 else jnp.where(odd, k, q)

        for h in range(GH):
            sl = slice(h * GDK, (h + 1) * GDK)
            vs = slice(h * GDV, (h + 1) * GDV)
            q = q_ref[:, sl].astype(F32) * (GDK ** -0.5)
            k = k_ref[:, sl].astype(F32)
            v = v_ref[:, vs]
            ex_h = ex_mm[:, sl]
            bcum = ex_h[0:GCH]
            e_cum = jnp.exp2(bcum)
            e_rem = jnp.exp2((bcum[GCH - 1:GCH] if d == 0 else bcum[0:1]) - bcum)
            st = s_scr[d, h]
            o = _dot_nt((q * e_cum).astype(BF16), st.astype(BF16))
            att = m_ref[d, G_LEVELS] * _dot_nt(q.astype(BF16), k.astype(BF16))
            for lv in range(G_LEVELS):
                z = (zsel(q, k, lv) * jnp.exp2(level_exponent(bcum, ex_h, lv))).astype(BF16)
                att = att + m_ref[d, lv] * _dot_nt(z, z)
            o = o + _dot(att.astype(BF16), v)
            o_ref[:, vs] = o.astype(o_ref.dtype)
            s_scr[d, h] = st * e_cum[last:last + 1, :] + _dot_tn(v, (k * e_rem).astype(BF16))


def _gla_scan(p, a, wa2p, ba, slot, amat, masks):
    def rowblk(b, m):
        return jnp.where(m < GC_CH, b * GC_CH + m, RC // GCH + b * (L // GCH) + (m - GC_CH))

    def bwd(c):
        return jnp.where(c < GC_CH, GC_CH - 1 - c, GN_CH + GC_CH - 1 - c)

    hk = GH * GDK
    hv = GH * GDV

    def specs(order):
        return [
            pl.BlockSpec((GCH, hk), lambda b, c: (rowblk(b, order(c)), 0)),
            pl.BlockSpec((GCH, hk), lambda b, c: (rowblk(b, order(c)), 1)),
            pl.BlockSpec((GCH, hv), lambda b, c: (rowblk(b, order(c)), 1)),
            pl.BlockSpec((GCH, 2 * GRANK), lambda b, c: (rowblk(b, order(c)), 0)),
        ]

    in_specs = specs(lambda c: c) + specs(bwd) + [
        pl.BlockSpec((2, 2 * GRANK, hk), lambda b, c: (0, 0, 0)),
        pl.BlockSpec((1, 2, hk), lambda b, c: (slot, 0, 0)),
        pl.BlockSpec((2, G_ROWS, 2 * GCH), lambda b, c: (0, 0, 0)),
        pl.BlockSpec((2, G_LEVELS + 1, GCH, GCH), lambda b, c: (0, 0, 0, 0)),
    ]
    out_specs = [
        pl.BlockSpec((GCH, hv), lambda b, c: (rowblk(b, c), 0)),
        pl.BlockSpec((GCH, hv), lambda b, c: (rowblk(b, bwd(c)), 0)),
    ]

    def body(qf, kf, vf, af, qb, kb, vb, ab, wa_ref, ba_ref, a_ref, m_ref, of_ref, ob_ref, s_scr):
        _gla_body(qf, kf, vf, af, qb, kb, vb, ab, wa_ref, ba_ref.at[0], a_ref, m_ref, of_ref, ob_ref, s_scr)

    return pl.pallas_call(
        body,
        grid=(B, GN_CH),
        in_specs=in_specs,
        out_specs=out_specs,
        out_shape=[jax.ShapeDtypeStruct((R, hv), BF16)] * 2,
        scratch_shapes=[pltpu.VMEM((2, GH, GDV, GDK), F32)],
        compiler_params=_cparams(("parallel", "arbitrary")),
        name="gla_scan",
    )(p, p, p, a, p, p, p, a, wa2p, ba, amat, masks)


def _swa_body(sink_ref, q_ref, kc_ref, vc_ref, k0, k1, k2, v0, v1, v2, o_ref):
    j = pl.program_id(1)
    is_ctx = j < LC // QB
    jl = j - LC // QB
    lo = jnp.where(is_ctx, 0, jnp.where(jl >= 1, 0, QB))
    hi = jnp.where(is_ctx, 0, jnp.where(jl <= L // QB - 2, 3 * QB, 2 * QB))
    nk = LC + 3 * QB
    rr = lax.broadcasted_iota(jnp.int32, (QB, nk), 0)
    tt = lax.broadcasted_iota(jnp.int32, (QB, nk), 1) - LC
    dlt = tt - rr
    valid = (tt < 0) | ((dlt >= 0) & (dlt <= 2 * WIN) & (tt >= lo) & (tt < hi))
    bias = jnp.where(valid, 0.0, NEG_INF)

    kcat = jnp.concatenate([kc_ref[...], k0[...], k1[...], k2[...]], axis=0).astype(F32)
    vcat = jnp.concatenate([vc_ref[...], v0[...], v1[...], v2[...]], axis=0).astype(F32)
    lane_q = lax.broadcasted_iota(jnp.int32, (QB, 128), 1)

    def dup(xp, half):
        low = lax.broadcasted_iota(jnp.int32, xp.shape, 1) < HD
        rolled = pltpu.roll(xp, HD, 1)
        out = jnp.where(low, xp, rolled) if half == 0 else jnp.where(low, rolled, xp)
        return out.astype(BF16)

    for kh in range(NKV):
        ps = slice((kh // 2) * 128, (kh // 2 + 1) * 128)
        kd, vd = dup(kcat[:, ps], kh % 2), dup(vcat[:, ps], kh % 2)
        qs = []
        for g in range(GRP):
            h = kh * GRP + g
            blk = q_ref[:, (h // 2) * 128:(h // 2 + 1) * 128].astype(F32)
            keep = (lane_q < HD) if h % 2 == 0 else (lane_q >= HD)
            qs.append(jnp.where(keep, blk, 0.0).astype(BF16))
        qst = jnp.concatenate(qs, axis=0)
        logits = _dot_nt(qst, kd)
        es, invs = [], []
        for g in range(GRP):
            lg = logits[g * QB:(g + 1) * QB] + bias
            s = sink_ref[0, kh * GRP + g]
            m = jnp.maximum(jnp.max(lg, axis=-1, keepdims=True), s)
            e = jnp.exp(lg - m)
            den = jnp.sum(e, axis=-1, keepdims=True) + jnp.exp(s - m)
            es.append(e.astype(BF16))
            invs.append(1.0 / den)
        res = _dot(jnp.concatenate(es, axis=0), vd) * jnp.concatenate(invs, axis=0)
        for m2 in range(GRP // 2):
            even = res[(2 * m2) * QB:(2 * m2 + 1) * QB]
            odd = res[(2 * m2 + 1) * QB:(2 * m2 + 2) * QB]
            col = (kh * (GRP // 2) + m2) * 128
            o_ref[:, col:col + 128] = jnp.where(lane_q < HD, even, odd).astype(o_ref.dtype)


def _swa_attention(p, sink):
    ncb = LC // QB
    nlb = L // QB
    lat0 = RC // QB
    kcol = NQ // NKVD

    def qrow(b, j):
        return jnp.where(j < ncb, b * ncb + j, lat0 + b * nlb + (j - ncb))

    def wrow(off):
        def f(b, j):
            jl = jnp.clip(j - ncb + off, 0, nlb - 1)
            return lat0 + b * nlb + jl
        return f

    in_specs = [
        pl.BlockSpec(memory_space=pltpu.SMEM),
        pl.BlockSpec((QB, NQ), lambda b, j: (qrow(b, j), 0)),
        pl.BlockSpec((LC, NKVD), lambda b, j: (b, kcol)),
        pl.BlockSpec((LC, NKVD), lambda b, j: (b, kcol + 1)),
    ]
    for col in (kcol, kcol + 1):
        for off in (-1, 0, 1):
            in_specs.append(pl.BlockSpec((QB, NKVD), functools.partial(
                lambda b, j, f, cc: (f(b, j), cc), f=wrow(off), cc=col)))
    return pl.pallas_call(
        _swa_body,
        grid=(B, ncb + nlb),
        in_specs=in_specs,
        out_specs=pl.BlockSpec((QB, NQ), lambda b, j: (qrow(b, j), 0)),
        out_shape=jax.ShapeDtypeStruct((R, NQ), BF16),
        compiler_params=_cparams(("parallel", "parallel")),
        name="swa_attn",
    )(sink, p, p, p, p, p, p, p, p, p)


def _gla_prologue(rows, of_ref, ob_ref, og_ref, on_ref):
    o = of_ref[rows, :].astype(F32) + ob_ref[rows, :].astype(F32)
    parts = [_rms(o[:, h * GDV:(h + 1) * GDV], on_ref[0, :, h * GDV:(h + 1) * GDV]) for h in range(GH)]
    return (jnp.concatenate(parts, axis=-1) * _silu(og_ref[rows, :].astype(F32))).astype(BF16)


def _swa_prologue(rows, o_ref):
    return o_ref[rows, :]


def _gmlp_prologue(rows, u_ref, v_ref, lng_ref, lnb_ref, ws_ref, bsb_ref, lhs_scr):
    v = v_ref[rows, :].astype(F32)
    mu = jnp.mean(v, axis=-1, keepdims=True)
    var = jnp.mean(jnp.square(v - mu), axis=-1, keepdims=True)
    vn = ((v - mu) * lax.rsqrt(var + EPS) * lng_ref[0] + lnb_ref[0]).astype(BF16)
    for ch in range(v.shape[0] // MC):
        rs = slice(ch * MC, (ch + 1) * MC)
        rg = slice(rows.start + ch * MC, rows.start + (ch + 1) * MC)
        for g in range(MG):
            cs = slice(g * 128, (g + 1) * 128)
            mixed = _dot(ws_ref[0, g], vn[rs, cs]) + bsb_ref[:, cs]
            lhs_scr[rg, cs] = (u_ref[rg, cs].astype(F32) * mixed).astype(BF16)
    return lhs_scr[rows, :]


OUT_SPLIT = 2


def _out_body(prologue, n_in, dual, *refs):
    in_refs = refs[:n_in]
    rest = list(refs[n_in:])
    wo_ref = rest.pop(0)
    if dual:
        load_x = _two_source_rows(pl.program_id(0) < RC // TM_OUT, rest.pop(0), rest.pop(0))
    else:
        x_ref = rest.pop(0)
        load_x = lambda rows: x_ref[rows, :]
    mod_ref, ng_ref, o_ref = rest[:3]
    piece = o_ref.shape[0] // OUT_SPLIT
    for r in range(OUT_SPLIT):
        rows = slice(r * piece, (r + 1) * piece)
        y = _dot(prologue(rows, *in_refs, *rest[3:]), wo_ref[0])
        o_ref[rows, :] = load_x(rows) + mod_ref[0, 0, 2:3, :] * _rms(y, ng_ref[0, 1:2, :])


def _out_proj(prologue, ins, in_specs, wo, slot, S, mods, norm_g, layer, *, skip_ctx, scratch=(),
              name="out_proj"):
    tm = TM_OUT
    off = RC // tm if skip_ctx else 0
    rows = RX if skip_ctx else R
    specs = [pl.BlockSpec(bs, functools.partial(lambda i, f: f(i + off), f=f)) for bs, f in in_specs]
    dual = isinstance(S, tuple)
    nct = RC // tm
    if dual:
        assert not skip_ctx
        row_specs = [pl.BlockSpec((tm, D), lambda i: (jnp.minimum(i, nct - 1), 0), pipeline_mode=pl.Buffered(1)),
                     pl.BlockSpec((tm, D), lambda i: (jnp.maximum(i - nct, 0), 0))]
    else:
        row_specs = [pl.BlockSpec((tm, D), lambda i: (i + off, 0))]
    specs += [pl.BlockSpec((1, D, D), lambda i: (slot, 0, 0), pipeline_mode=pl.Buffered(1))] + row_specs + [
        pl.BlockSpec((1, 1, 6, D), lambda i: (layer, _mod_row(i + off, tm), 0, 0)),
        pl.BlockSpec((1, 4, D), lambda i: (layer, 0, 0)),
    ]
    return pl.pallas_call(
        functools.partial(_out_body, prologue, len(ins), dual),
        grid=(rows // tm,),
        in_specs=specs,
        out_specs=pl.BlockSpec((tm, D), lambda i: (i, 0)),
        out_shape=jax.ShapeDtypeStruct((rows, D), F32),
        scratch_shapes=list(scratch),
        compiler_params=_cparams(("parallel",)),
        name=name,
    )(*ins, wo, *(S if dual else (S,)), mods, norm_g)


def _ffn_body(nf, x_ref, mod_ref, ng_ref, w1_ref, w3_ref, w2_ref, o_ref, h_scr, rs_scr):
    f = pl.program_id(1)

    @pl.when(f == 0)
    def _():
        _row_rsqrt(lambda rows: x_ref[rows, :], rs_scr)
        gsc = ng_ref[0, 2:3, :] * (1.0 + mod_ref[0, 0, 4:5, :])

        def pro(rows):
            h = x_ref[rows, :] * _lanes(rs_scr[rows, :], D) * gsc + mod_ref[0, 0, 3:4, :]
            h_scr[rows, :] = h.astype(BF16)
        _for_row_chunks(x_ref.shape[0], pro)

    def partial_sum():
        hb = h_scr[...]
        a = _dot(hb, w1_ref[0].astype(BF16))
        b = _dot(hb, w3_ref[0].astype(BF16))
        return _dot((_silu(a) * b).astype(BF16), w2_ref[0].astype(BF16))

    @pl.when(f == 0)
    def _():
        o_ref[...] = partial_sum()

    @pl.when(f > 0)
    def _():
        o_ref[...] += partial_sum()

    @pl.when(f == nf - 1)
    def _():
        _row_rsqrt(lambda rows: o_ref[rows, :], rs_scr)
        gg = mod_ref[0, 0, 5:6, :] * ng_ref[0, 3:4, :]

        def epi(rows):
            o_ref[rows, :] = x_ref[rows, :] + o_ref[rows, :] * _lanes(rs_scr[rows, :], D) * gg
        _for_row_chunks(x_ref.shape[0], epi)


def _ffn(S, mods, norm_g, w1, w3, w2, layer):
    tm, tf = TM_FFN, TF_FFN
    nf = FF // tf
    off = (R - S.shape[0]) // tm
    return pl.pallas_call(
        functools.partial(_ffn_body, nf),
        grid=(S.shape[0] // tm, nf),
        in_specs=[
            pl.BlockSpec((tm, D), lambda i, f: (i, 0)),
            pl.BlockSpec((1, 1, 6, D), lambda i, f: (layer, _mod_row(i + off, tm), 0, 0)),
            pl.BlockSpec((1, 4, D), lambda i, f: (layer, 0, 0)),
            pl.BlockSpec((1, D, tf), lambda i, f: (layer, 0, f)),
            pl.BlockSpec((1, D, tf), lambda i, f: (layer, 0, f)),
            pl.BlockSpec((1, tf, D), lambda i, f: (layer, f, 0)),
        ],
        out_specs=pl.BlockSpec((tm, D), lambda i, f: (i, 0)),
        out_shape=jax.ShapeDtypeStruct(S.shape, F32),
        scratch_shapes=[pltpu.VMEM((tm, D), BF16), pltpu.VMEM((tm, 128), F32)],
        compiler_params=_cparams(("parallel", "arbitrary")),
        name="ffn",
    )(S, mods, norm_g, w1, w3, w2)


def _rope_tables():
    quarter = HD // 4
    inv_freq = 10000.0 ** (-jnp.arange(quarter, dtype=F32) / quarter)
    row = jnp.repeat(jnp.arange(L // 64), 64).astype(F32)
    col = jnp.tile(jnp.arange(64), L // 64).astype(F32)
    ang_r = row[:, None] * inv_freq
    ang_c = col[:, None] * inv_freq
    ang = jnp.concatenate([ang_r, ang_r, ang_c, ang_c], axis=-1)
    reps = SUB_N // HD
    cos, sin = jnp.tile(jnp.cos(ang), (1, reps)), jnp.tile(jnp.sin(ang), (1, reps))
    first = (jnp.arange(SUB_N) % 32) < 16
    return cos, jnp.stack([jnp.where(first, -sin, 0.0), jnp.where(first, 0.0, sin)])


def kernel(x, c, ctx, c_ctx, ada_w, ada_b, norm_g, ffn_w1, ffn_w3, ffn_w2,
           gla_w_in, gla_wa2, gla_ba, gla_onorm_g, gla_wo,
           attn_w_in, attn_sink, attn_wo,
           gmlp_w_in, gmlp_ln_g, gmlp_ln_b, gmlp_ws, gmlp_bs, gmlp_wo):
    S = (ctx.reshape(RC, D), x.reshape(RX, D))
    cvec = jnp.concatenate([c_ctx[None, :], c, jnp.zeros((3, D), F32)], axis=0)
    mods = _modulation(cvec, ada_w, ada_b).reshape(DEPTH, 8, 6, D)
    rowtile = lambda i: (i, 0)
    w1, w3, w2 = ffn_w1, ffn_w3, ffn_w2
    gla_w, gla_wo_b = jnp.swapaxes(gla_w_in, 1, 2), gla_wo.astype(BF16)
    hk, hv = GH * GDK, GH * GDV

    for i in range(DEPTH):
        last = i == DEPTH - 1
        kind, slot = i % 3, i // 3
        if kind == 0:
            p, a = _proj(S, mods, norm_g, i, gla_w, slot, 2 * hk + 2 * hv, w_t=True,
                         small_rows=2 * GRANK, name="gla_proj")
            wa2 = gla_wa2[slot]
            wa2p = jnp.zeros((2, 2 * GRANK, hk), F32)
            wa2p = wa2p.at[0, :GRANK].set(wa2[0]).at[1, GRANK:].set(wa2[1]).astype(BF16)
            amat, masks = _gla_constants()
            o_f, o_b = _gla_scan(p, a, wa2p, gla_ba, slot, amat, masks)
            ins = [o_f, o_b, p, gla_onorm_g.reshape(-1, 1, hv)]
            in_specs = [((TM_OUT, hv), rowtile), ((TM_OUT, hv), rowtile),
                        ((TM_OUT, hv), lambda i: (i, 2)),
                        ((1, 1, hv), functools.partial(lambda i, s: (s, 0, 0), s=slot))]
            S = _out_proj(_gla_prologue, ins, in_specs, gla_wo_b, slot, S, mods, norm_g, i,
                          skip_ctx=last, name="gla_out")
        elif kind == 1:
            cos, sin = _rope_tables()
            p = _proj(S, mods, norm_g, i, attn_w_in, slot, NQ + 2 * NKVD, kind="rope",
                      cos=cos, sin=sin, name="swa_proj")
            o = _swa_attention(p, attn_sink[slot:slot + 1])
            S = _out_proj(_swa_prologue, [o], [((TM_OUT, NQ), rowtile)], attn_wo.astype(BF16), slot,
                          S, mods, norm_g, i, skip_ctx=last, name="swa_out")
        else:
            p = _proj(S, mods, norm_g, i, gmlp_w_in, slot, 2 * D, kind="gelu",
                      name="gmlp_proj")
            bsb = jnp.repeat(gmlp_bs[slot].T, 128, axis=1)
            sel = functools.partial(lambda i, s: (s, 0, 0), s=slot)
            ins = [p, p, gmlp_ln_g.reshape(-1, 1, D), gmlp_ln_b.reshape(-1, 1, D),
                   gmlp_ws.astype(BF16), bsb]
            in_specs = [((TM_OUT, D), rowtile), ((TM_OUT, D), lambda i: (i, 1)),
                        ((1, 1, D), sel), ((1, 1, D), sel),
                        ((1, MG, MC, MC), functools.partial(lambda i, s: (s, 0, 0, 0), s=slot)),
                        ((MC, D), lambda i: (0, 0))]
            S = _out_proj(_gmlp_prologue, ins, in_specs, gmlp_wo.astype(BF16), slot, S, mods, norm_g, i,
                          skip_ctx=last, scratch=[pltpu.VMEM((TM_OUT, D), BF16)], name="gmlp_out")
        S = _ffn(S, mods, norm_g, w1, w3, w2, i)
    return S.reshape(B, L, D)
```

```python
import functools

import numpy as np
import jax
import jax.numpy as jnp
from jax import lax
from jax.experimental import pallas as pl
from jax.experimental.pallas import tpu as pltpu

F32 = jnp.float32
BF16 = jnp.bfloat16

D = 2048
B = 4
L = 2048
LC = 256
DEPTH = 4
FF = 5632
EPS = 1e-6
NEG_INF = -1e30
LOG2E = 1.4426950408889634

RC = B * LC
RX = B * L
R = RC + RX

GH = 4
GDK = 256
GDV = 512
GRANK = 16
GCH = 128
GN_CH = (LC + L) // GCH
GC_CH = LC // GCH
G_LEVELS = 7
G_SMALL = 3
G_ROWS = (1 + G_SMALL) * GCH

HD = 64
NH = 32
NKV = 4
GRP = 8
WIN = 128
QB = 128
NQ = NH * HD
NKVD = NKV * HD

MC = 128
MG = 16

TM_PROJ = 1024
TN_PROJ = (1024, 512)
TM_OUT = 512
TM_FFN = 1024
TF_FFN = 256

VMEM_LIMIT = 56 * 1024 * 1024


def _cparams(sem, vmem=VMEM_LIMIT):
    return pltpu.CompilerParams(dimension_semantics=sem, vmem_limit_bytes=vmem)


def _mod_row(i, tm):
    nct = RC // tm
    return jnp.where(i < nct, 0, 1 + (i - nct) // (L // tm))


def _norm_mod(x, ng, sc, sh):
    ms = jnp.mean(x * x, axis=-1, keepdims=True)
    return (x * lax.rsqrt(ms + EPS) * ng) * (1.0 + sc) + sh


def _rms(y, g):
    ms = jnp.mean(y * y, axis=-1, keepdims=True)
    return y * lax.rsqrt(ms + EPS) * g


ROW_CHUNK = 32


def _for_row_chunks(n_rows, fn, unroll=4):
    def body(r, carry):
        fn(pl.ds(pl.multiple_of(r * ROW_CHUNK, ROW_CHUNK), ROW_CHUNK))
        return carry
    lax.fori_loop(0, n_rows // ROW_CHUNK, body, 0, unroll=unroll)


def _row_rsqrt(load, rs_scr):
    def body(rows):
        x = load(rows)
        ms = jnp.mean(x * x, axis=-1, keepdims=True)
        rs_scr[rows, :] = jnp.broadcast_to(lax.rsqrt(ms + EPS), (ROW_CHUNK, 128))
    _for_row_chunks(rs_scr.shape[0], body, unroll=8)


def _lanes(rs, width):
    return jnp.concatenate([rs] * (width // 128), axis=1)


def _silu(a):
    return a * jax.nn.sigmoid(a)


def _dot(a, b):
    return jnp.dot(a, b, preferred_element_type=F32)


def _dot_nt(a, b):
    return lax.dot_general(a, b, (((1,), (1,)), ((), ())), preferred_element_type=F32)


def _dot_tn(a, b):
    return lax.dot_general(a, b, (((0,), (0,)), ((), ())), preferred_element_type=F32)


def _mod_body(c_ref, w_ref, b_ref, o_ref):
    s = _silu(c_ref[...]).astype(BF16)
    o_ref[0] = _dot(s, w_ref[0].astype(BF16)) + b_ref[0]


def _modulation(cvec, ada_w, ada_b, n_layers):
    tn = 1024
    n6 = 6 * D
    return pl.pallas_call(
        _mod_body,
        grid=(n_layers, n6 // tn),
        in_specs=[
            pl.BlockSpec((8, D), lambda l, n: (0, 0)),
            pl.BlockSpec((1, D, tn), lambda l, n: (l, 0, n)),
            pl.BlockSpec((1, 1, tn), lambda l, n: (l, 0, n)),
        ],
        out_specs=pl.BlockSpec((1, 8, tn), lambda l, n: (l, 0, n)),
        out_shape=jax.ShapeDtypeStruct((n_layers, 8, n6), F32),
        compiler_params=_cparams(("parallel", "parallel")),
        name="adaln_mod",
    )(cvec, ada_w, ada_b.reshape(DEPTH, 1, n6))


SUB_N = 256
SUB_M = 256


def _two_source_rows(is_ctx, c_ref, x_ref):
    return lambda rows: jnp.where(is_ctx, c_ref[rows, :], x_ref[rows, :])


def _proj_body(kind, small_rows, w_t, dual, tm, tn, *refs):
    rest = list(refs)
    if dual:
        load_x = _two_source_rows(pl.program_id(0) < RC // tm, rest.pop(0), rest.pop(0))
    else:
        x_ref = rest.pop(0)
        load_x = lambda rows: x_ref[rows, :]
    mod_ref, ng_ref, w_ref = rest.pop(0), rest.pop(0), rest.pop(0)
    if kind == "rope":
        cos_ref, sin_ref = rest.pop(0), rest.pop(0)
    if small_rows:
        ws_ref = rest.pop(0)
    o_ref = rest.pop(0)
    if small_rows:
        os_ref = rest.pop(0)
    h_scr, rs_scr = rest.pop(0), rest.pop(0)
    i = pl.program_id(0)
    n = pl.program_id(1)

    @pl.when(n == 0)
    def _():
        _row_rsqrt(load_x, rs_scr)
        gsc = ng_ref[0, 0:1, :] * (1.0 + mod_ref[0, 0, 1:2, :])

        def pro(rows):
            h = load_x(rows) * _lanes(rs_scr[rows, :], D) * gsc + mod_ref[0, 0, 0:1, :]
            h_scr[rows, :] = h.astype(BF16)
        _for_row_chunks(tm, pro)
        if small_rows:
            os_ref[...] = _dot_nt(h_scr[...], ws_ref[0].astype(BF16)).astype(os_ref.dtype)

    sub_m = SUB_M if kind == "rope" else tm
    for c, r in [(c, r) for c in range(tn // SUB_N) for r in range(tm // sub_m)]:
        cs = slice(c * SUB_N, (c + 1) * SUB_N)
        rs = slice(r * sub_m, (r + 1) * sub_m)
        hb = h_scr[rs, :]
        if w_t:
            acc = _dot_nt(hb, w_ref[0, cs, :].astype(BF16))
        else:
            acc = _dot(hb, w_ref[0, :, cs].astype(BF16))
        if kind == "none":
            o_ref[rs, cs] = acc.astype(o_ref.dtype)
        elif kind == "gelu":
            o_ref[rs, cs] = jax.nn.gelu(acc, approximate=True).astype(o_ref.dtype)
        else:
            piece = n * (tn // SUB_N) + c
            scale = jnp.where(piece < NQ // SUB_N, HD ** -0.5, 1.0).astype(F32)
            rotate = jnp.logical_and(i >= RC // tm, piece < (NQ + NKVD) // SUB_N)

            roped = (acc * cos_ref[rs, :] + pltpu.roll(acc, SUB_N - 16, 1) * sin_ref[0, rs, :]
                     + pltpu.roll(acc, 16, 1) * sin_ref[1, rs, :])
            o_ref[rs, cs] = (jnp.where(rotate, roped, acc) * scale).astype(o_ref.dtype)


def _proj(S, mods, norm_g, layer, w, slot, n_cols, *, kind="none", w_t=False, small_rows=0,
          cos=None, sin=None, name="proj"):
    tm = TM_PROJ
    tn = max(t for t in TN_PROJ if n_cols % t == 0)
    nct = RC // tm
    mods, mlayer = mods
    once = pl.Buffered(1)
    dual = isinstance(S, tuple)
    if dual:
        row_specs = [pl.BlockSpec((tm, D), lambda i, n: (jnp.minimum(i, nct - 1), 0), pipeline_mode=once),
                     pl.BlockSpec((tm, D), lambda i, n: (jnp.maximum(i - nct, 0), 0))]
    else:
        row_specs = [pl.BlockSpec((tm, D), lambda i, n: (i, 0))]
    in_specs = row_specs + [
        pl.BlockSpec((1, 1, 6, D), lambda i, n: (mlayer, _mod_row(i, tm), 0, 0)),
        pl.BlockSpec((1, 4, D), lambda i, n: (layer, 0, 0)),
        (pl.BlockSpec((1, tn, D), lambda i, n: (slot, n, 0)) if w_t
         else pl.BlockSpec((1, D, tn), lambda i, n: (slot, 0, n))),
    ]
    args = (list(S) if dual else [S]) + [mods, norm_g, w]
    if kind == "rope":
        pos = lambda i, n: jnp.where(i < nct, 0, (i - nct) % (L // tm))
        in_specs += [pl.BlockSpec((tm, SUB_N), lambda i, n: (pos(i, n), 0)),
                     pl.BlockSpec((2, tm, SUB_N), lambda i, n: (0, pos(i, n), 0))]
        args += [cos, sin]
    out_specs = [pl.BlockSpec((tm, tn), lambda i, n: (i, n))]
    out_shape = [jax.ShapeDtypeStruct((R, n_cols), BF16)]
    if small_rows:
        assert w_t and n_cols % small_rows == 0
        in_specs.append(pl.BlockSpec((1, small_rows, D), lambda i, n: (slot, n_cols // small_rows, 0)))
        args.append(w)
        out_specs.append(pl.BlockSpec((tm, small_rows), lambda i, n: (i, 0)))
        out_shape.append(jax.ShapeDtypeStruct((R, small_rows), BF16))
    res = pl.pallas_call(
        functools.partial(_proj_body, kind, small_rows, w_t, dual, tm, tn),
        grid=(R // tm, n_cols // tn),
        in_specs=in_specs,
        out_specs=out_specs,
        out_shape=out_shape,
        scratch_shapes=[pltpu.VMEM((tm, D), BF16), pltpu.VMEM((tm, 128), F32)],
        compiler_params=_cparams(("parallel", "arbitrary")),
        name=name,
    )(*args)
    return res if small_rows else res[0]


def _gla_constants():
    c = GCH
    r = np.arange(c)[:, None]
    t = np.arange(c)[None, :]
    blocks = [(t <= r)]
    masks = []
    for lv in range(G_LEVELS):
        s = c >> (lv + 1)
        m = r // s
        odd = (m % 2) == 1
        if s < 8:
            a_odd = (t > s * m) & (t <= r)
            a_even = (t > r) & (t <= s * (m + 1))
            blocks.append(np.where(odd, a_odd, a_even))
        masks.append(odd & ((t // s) == m - 1))
    masks.append(r == t)
    a_f = np.concatenate(blocks, axis=0).astype(np.float32)
    m_f = np.stack(masks).astype(np.float32)
    a_b = np.concatenate([blk[::-1, ::-1] for blk in blocks], axis=0).astype(np.float32)
    m_b = m_f[:, ::-1, ::-1]
    amat = np.stack([a_f, a_b])
    amat = np.concatenate([amat, amat], axis=2)
    return jnp.asarray(amat, BF16), jnp.asarray(np.stack([m_f, m_b]), F32)


def _gla_body(ctx_out, qf, kf, vf, af, qb, kb, vb, ab, wa_ref, ba_ref, a_ref, m_ref, of_ref, ob_ref, s_scr):
    c = pl.program_id(1)

    @pl.when(c == 0)
    def _():
        s_scr[...] = jnp.zeros_like(s_scr)

    row = lax.broadcasted_iota(jnp.int32, (GCH, GDK), 0)
    odd_rows = {s: ((row // s) % 2) == 1 for s in (4, 2, 1)}
    dirs = ((qf, kf, vf, af, of_ref, GCH - 1), (qb, kb, vb, ab, ob_ref, 0))

    def process(with_out):
        for d, (q_ref, k_ref, v_ref, ar_ref, o_ref, last) in enumerate(dirs):
            araw = _dot(ar_ref[...], wa_ref[d]) + ba_ref[d:d + 1, :]
            g = (jnp.minimum(araw, 0.0) - jnp.log1p(jnp.exp(-jnp.abs(araw)))) * (LOG2E / 16.0)
            g1 = g.astype(BF16)
            g2 = (g - g1.astype(F32)).astype(BF16)
            amat = a_ref[d] if with_out else a_ref[d, 0:GCH, :]
            ex_mm = _dot(amat, jnp.concatenate([g1, g2], axis=0))

            def level_exponent(bcum, ex_h, lv):
                s = GCH >> (lv + 1)
                if s < 8:
                    n_big = G_LEVELS - G_SMALL
                    return ex_h[(1 + lv - n_big) * GCH:(2 + lv - n_big) * GCH]
                parts = []
                for p in range(GCH // (2 * s)):
                    lo = 2 * p * s
                    ev, od = bcum[lo:lo + s], bcum[lo + s:lo + 2 * s]
                    if d == 0:
                        ref = bcum[lo + s:lo + s + 1]
                        parts += [ref - ev, od - ref]
                    else:
                        ref = bcum[lo + s - 1:lo + s]
                        parts += [ev - ref, ref - od]
                return jnp.concatenate(parts, axis=0)

            def zsel(q, k, lv):
                s = GCH >> (lv + 1)
                if s >= 8:
                    parts = [(q if (m % 2 == 1) == (d == 0) else k)[m * s:(m + 1) * s]
                             for m in range(GCH // s)]
                    return jnp.concatenate(parts, axis=0)
                return jnp.where(odd_rows[s], q, k) if d == 0 else jnp.where(odd_rows[s], k, q)

            if not with_out:
                o_ref[...] = jnp.zeros_like(o_ref)
            for h in range(GH):
                sl = slice(h * GDK, (h + 1) * GDK)
                vs = slice(h * GDV, (h + 1) * GDV)
                k = k_ref[:, sl].astype(F32)
                v = v_ref[:, vs]
                ex_h = ex_mm[:, sl]
                bcum = ex_h[0:GCH]
                e_cum = jnp.exp2(bcum)
                e_rem = jnp.exp2((bcum[GCH - 1:GCH] if d == 0 else bcum[0:1]) - bcum)
                st = s_scr[d, h]
                if with_out:
                    q = q_ref[:, sl].astype(F32) * (GDK ** -0.5)
                    o = _dot_nt((q * e_cum).astype(BF16), st.astype(BF16))
                    att = m_ref[d, G_LEVELS] * _dot_nt(q.astype(BF16), k.astype(BF16))
                    for lv in range(G_LEVELS):
                        z = (zsel(q, k, lv) * jnp.exp2(level_exponent(bcum, ex_h, lv))).astype(BF16)
                        att = att + m_ref[d, lv] * _dot_nt(z, z)
                    o = o + _dot(att.astype(BF16), v)
                    o_ref[:, vs] = o.astype(o_ref.dtype)
                s_scr[d, h] = st * e_cum[last:last + 1, :] + _dot_tn(v, (k * e_rem).astype(BF16))

    if ctx_out:
        process(True)
    else:
        pl.when(c >= GC_CH)(lambda: process(True))
        pl.when(c < GC_CH)(lambda: process(False))


def _gla_scan(p, a, wa2p, ba, slot, amat, masks, ctx_out):
    def rowblk(b, m):
        return jnp.where(m < GC_CH, b * GC_CH + m, RC // GCH + b * (L // GCH) + (m - GC_CH))

    def bwd(c):
        return jnp.where(c < GC_CH, GC_CH - 1 - c, GN_CH + GC_CH - 1 - c)

    hk = GH * GDK
    hv = GH * GDV

    def specs(order):
        return [
            pl.BlockSpec((GCH, hk), lambda b, c: (rowblk(b, order(c)), 0)),
            pl.BlockSpec((GCH, hk), lambda b, c: (rowblk(b, order(c)), 1)),
            pl.BlockSpec((GCH, hv), lambda b, c: (rowblk(b, order(c)), 1)),
            pl.BlockSpec((GCH, 2 * GRANK), lambda b, c: (rowblk(b, order(c)), 0)),
        ]

    in_specs = specs(lambda c: c) + specs(bwd) + [
        pl.BlockSpec((2, 2 * GRANK, hk), lambda b, c: (0, 0, 0)),
        pl.BlockSpec((1, 2, hk), lambda b, c: (slot, 0, 0)),
        pl.BlockSpec((2, G_ROWS, 2 * GCH), lambda b, c: (0, 0, 0)),
        pl.BlockSpec((2, G_LEVELS + 1, GCH, GCH), lambda b, c: (0, 0, 0, 0)),
    ]
    out_specs = [
        pl.BlockSpec((GCH, hv), lambda b, c: (rowblk(b, c), 0)),
        pl.BlockSpec((GCH, hv), lambda b, c: (rowblk(b, bwd(c)), 0)),
    ]

    def body(qf, kf, vf, af, qb, kb, vb, ab, wa_ref, ba_ref, a_ref, m_ref, of_ref, ob_ref, s_scr):
        _gla_body(ctx_out, qf, kf, vf, af, qb, kb, vb, ab, wa_ref, ba_ref.at[0], a_ref, m_ref, of_ref, ob_ref, s_scr)

    return pl.pallas_call(
        body,
        grid=(B, GN_CH),
        in_specs=in_specs,
        out_specs=out_specs,
        out_shape=[jax.ShapeDtypeStruct((R, hv), BF16)] * 2,
        scratch_shapes=[pltpu.VMEM((2, GH, GDV, GDK), F32)],
        compiler_params=_cparams(("parallel", "arbitrary")),
        name="gla_scan",
    )(p, p, p, a, p, p, p, a, wa2p, ba, amat, masks)


def _swa_body(sink_ref, q_ref, kc_ref, vc_ref, k0, k1, k2, v0, v1, v2, o_ref):
    j = pl.program_id(1)
    is_ctx = j < LC // QB
    jl = j - LC // QB
    lo = jnp.where(is_ctx, 0, jnp.where(jl >= 1, 0, QB))
    hi = jnp.where(is_ctx, 0, jnp.where(jl <= L // QB - 2, 3 * QB, 2 * QB))
    nk = LC + 3 * QB
    rr = lax.broadcasted_iota(jnp.int32, (QB, nk), 0)
    tt = lax.broadcasted_iota(jnp.int32, (QB, nk), 1) - LC
    dlt = tt - rr
    valid = (tt < 0) | ((dlt >= 0) & (dlt <= 2 * WIN) & (tt >= lo) & (tt < hi))
    bias = jnp.where(valid, 0.0, NEG_INF)

    kcat = jnp.concatenate([kc_ref[...], k0[...], k1[...], k2[...]], axis=0).astype(F32)
    vcat = jnp.concatenate([vc_ref[...], v0[...], v1[...], v2[...]], axis=0).astype(F32)
    lane_q = lax.broadcasted_iota(jnp.int32, (QB, 128), 1)

    def dup(xp, half):
        low = lax.broadcasted_iota(jnp.int32, xp.shape, 1) < HD
        rolled = pltpu.roll(xp, HD, 1)
        out = jnp.where(low, xp, rolled) if half == 0 else jnp.where(low, rolled, xp)
        return out.astype(BF16)

    for kh in range(NKV):
        ps = slice((kh // 2) * 128, (kh // 2 + 1) * 128)
        kd, vd = dup(kcat[:, ps], kh % 2), dup(vcat[:, ps], kh % 2)
        qs = []
        for g in range(GRP):
            h = kh * GRP + g
            blk = q_ref[:, (h // 2) * 128:(h // 2 + 1) * 128].astype(F32)
            keep = (lane_q < HD) if h % 2 == 0 else (lane_q >= HD)
            qs.append(jnp.where(keep, blk, 0.0).astype(BF16))
        qst = jnp.concatenate(qs, axis=0)
        logits = _dot_nt(qst, kd)
        es, invs = [], []
        for g in range(GRP):
            lg = logits[g * QB:(g + 1) * QB] + bias
            s = sink_ref[0, kh * GRP + g]
            m = jnp.maximum(jnp.max(lg, axis=-1, keepdims=True), s)
            e = jnp.exp(lg - m)
            den = jnp.sum(e, axis=-1, keepdims=True) + jnp.exp(s - m)
            es.append(e.astype(BF16))
            invs.append(1.0 / den)
        res = _dot(jnp.concatenate(es, axis=0), vd) * jnp.concatenate(invs, axis=0)
        for m2 in range(GRP // 2):
            even = res[(2 * m2) * QB:(2 * m2 + 1) * QB]
            odd = res[(2 * m2 + 1) * QB:(2 * m2 + 2) * QB]
            col = (kh * (GRP // 2) + m2) * 128
            o_ref[:, col:col + 128] = jnp.where(lane_q < HD, even, odd).astype(o_ref.dtype)


def _swa_attention(p, sink):
    ncb = LC // QB
    nlb = L // QB
    lat0 = RC // QB
    kcol = NQ // NKVD

    def qrow(b, j):
        return jnp.where(j < ncb, b * ncb + j, lat0 + b * nlb + (j - ncb))

    def wrow(off):
        def f(b, j):
            jl = jnp.clip(j - ncb + off, 0, nlb - 1)
            return lat0 + b * nlb + jl
        return f

    in_specs = [
        pl.BlockSpec(memory_space=pltpu.SMEM),
        pl.BlockSpec((QB, NQ), lambda b, j: (qrow(b, j), 0)),
        pl.BlockSpec((LC, NKVD), lambda b, j: (b, kcol)),
        pl.BlockSpec((LC, NKVD), lambda b, j: (b, kcol + 1)),
    ]
    for col in (kcol, kcol + 1):
        for off in (-1, 0, 1):
            in_specs.append(pl.BlockSpec((QB, NKVD), functools.partial(
                lambda b, j, f, cc: (f(b, j), cc), f=wrow(off), cc=col)))
    return pl.pallas_call(
        _swa_body,
        grid=(B, ncb + nlb),
        in_specs=in_specs,
        out_specs=pl.BlockSpec((QB, NQ), lambda b, j: (qrow(b, j), 0)),
        out_shape=jax.ShapeDtypeStruct((R, NQ), BF16),
        compiler_params=_cparams(("parallel", "parallel")),
        name="swa_attn",
    )(sink, p, p, p, p, p, p, p, p, p)


def _gla_prologue(rows, of_ref, ob_ref, og_ref, on_ref):
    o = of_ref[rows, :].astype(F32) + ob_ref[rows, :].astype(F32)
    parts = [_rms(o[:, h * GDV:(h + 1) * GDV], on_ref[0, :, h * GDV:(h + 1) * GDV]) for h in range(GH)]
    return (jnp.concatenate(parts, axis=-1) * _silu(og_ref[rows, :].astype(F32))).astype(BF16)


def _swa_prologue(rows, o_ref):
    return o_ref[rows, :]


def _gmlp_prologue(rows, u_ref, v_ref, lng_ref, lnb_ref, ws_ref, bsb_ref, lhs_scr):
    v = v_ref[rows, :].astype(F32)
    mu = jnp.mean(v, axis=-1, keepdims=True)
    var = jnp.mean(jnp.square(v - mu), axis=-1, keepdims=True)
    vn = ((v - mu) * lax.rsqrt(var + EPS) * lng_ref[0] + lnb_ref[0]).astype(BF16)
    for ch in range(v.shape[0] // MC):
        rs = slice(ch * MC, (ch + 1) * MC)
        rg = slice(rows.start + ch * MC, rows.start + (ch + 1) * MC)
        for g in range(MG):
            cs = slice(g * 128, (g + 1) * 128)
            mixed = _dot(ws_ref[0, g], vn[rs, cs]) + bsb_ref[:, cs]
            lhs_scr[rg, cs] = (u_ref[rg, cs].astype(F32) * mixed).astype(BF16)
    return lhs_scr[rows, :]


OUT_SPLIT = 2


def _out_body(prologue, n_in, dual, *refs):
    in_refs = refs[:n_in]
    rest = list(refs[n_in:])
    wo_ref = rest.pop(0)
    if dual:
        load_x = _two_source_rows(pl.program_id(0) < RC // TM_OUT, rest.pop(0), rest.pop(0))
    else:
        x_ref = rest.pop(0)
        load_x = lambda rows: x_ref[rows, :]
    mod_ref, ng_ref, o_ref = rest[:3]
    piece = o_ref.shape[0] // OUT_SPLIT
    for r in range(OUT_SPLIT):
        rows = slice(r * piece, (r + 1) * piece)
        y = _dot(prologue(rows, *in_refs, *rest[3:]), wo_ref[0])
        o_ref[rows, :] = load_x(rows) + mod_ref[0, 0, 2:3, :] * _rms(y, ng_ref[0, 1:2, :])


def _out_proj(prologue, ins, in_specs, wo, slot, S, mods, norm_g, layer, *, skip_ctx, scratch=(),
              name="out_proj"):
    tm = TM_OUT
    mods, mlayer = mods
    off = RC // tm if skip_ctx else 0
    rows = RX if skip_ctx else R
    specs = [pl.BlockSpec(bs, functools.partial(lambda i, f: f(i + off), f=f)) for bs, f in in_specs]
    dual = isinstance(S, tuple)
    nct = RC // tm
    if dual:
        assert not skip_ctx
        row_specs = [pl.BlockSpec((tm, D), lambda i: (jnp.minimum(i, nct - 1), 0), pipeline_mode=pl.Buffered(1)),
                     pl.BlockSpec((tm, D), lambda i: (jnp.maximum(i - nct, 0), 0))]
    else:
        row_specs = [pl.BlockSpec((tm, D), lambda i: (i + off, 0))]
    specs += [pl.BlockSpec((1, D, D), lambda i: (slot, 0, 0), pipeline_mode=pl.Buffered(1))] + row_specs + [
        pl.BlockSpec((1, 1, 6, D), lambda i: (mlayer, _mod_row(i + off, tm), 0, 0)),
        pl.BlockSpec((1, 4, D), lambda i: (layer, 0, 0)),
    ]
    return pl.pallas_call(
        functools.partial(_out_body, prologue, len(ins), dual),
        grid=(rows // tm,),
        in_specs=specs,
        out_specs=pl.BlockSpec((tm, D), lambda i: (i, 0)),
        out_shape=jax.ShapeDtypeStruct((rows, D), F32),
        scratch_shapes=list(scratch),
        compiler_params=_cparams(("parallel",)),
        name=name,
    )(*ins, wo, *(S if dual else (S,)), mods, norm_g)


SIDE_TN = 256


def _ffn_body(nf, side, x_ref, mod_ref, ng_ref, w1_ref, w3_ref, w2_ref, *rest):
    f = pl.program_id(1)
    if side:
        c_ref, aw_ref, ab_ref, o_ref, mo_ref, h_scr, rs_scr = rest

        def side_job():
            mo_ref[0] = _dot(_silu(c_ref[...]).astype(BF16), aw_ref[0].astype(BF16)) + ab_ref[0]
    else:
        o_ref, h_scr, rs_scr = rest
        side_job = lambda: None

    @pl.when(f == 0)
    def _():
        _row_rsqrt(lambda rows: x_ref[rows, :], rs_scr)
        gsc = ng_ref[0, 2:3, :] * (1.0 + mod_ref[0, 0, 4:5, :])

        def pro(rows):
            h = x_ref[rows, :] * _lanes(rs_scr[rows, :], D) * gsc + mod_ref[0, 0, 3:4, :]
            h_scr[rows, :] = h.astype(BF16)
        _for_row_chunks(x_ref.shape[0], pro)

    def partial_sum():
        hb = h_scr[...]
        a = _dot(hb, w1_ref[0].astype(BF16))
        b = _dot(hb, w3_ref[0].astype(BF16))
        return _dot((_silu(a) * b).astype(BF16), w2_ref[0].astype(BF16))

    @pl.when(f == 0)
    def _():
        side_job()
        o_ref[...] = partial_sum()

    @pl.when(f > 0)
    def _():
        side_job()
        o_ref[...] += partial_sum()

    @pl.when(f == nf - 1)
    def _():
        _row_rsqrt(lambda rows: o_ref[rows, :], rs_scr)
        gg = mod_ref[0, 0, 5:6, :] * ng_ref[0, 3:4, :]

        def epi(rows):
            o_ref[rows, :] = x_ref[rows, :] + o_ref[rows, :] * _lanes(rs_scr[rows, :], D) * gg
        _for_row_chunks(x_ref.shape[0], epi)


def _ffn(S, mods, norm_g, w1, w3, w2, layer, adaln=None):
    tm, tf = TM_FFN, TF_FFN
    nf = FF // tf
    off = (R - S.shape[0]) // tm
    mods, mlayer = mods
    in_specs = [
        pl.BlockSpec((tm, D), lambda i, f: (i, 0)),
        pl.BlockSpec((1, 1, 6, D), lambda i, f: (mlayer, _mod_row(i + off, tm), 0, 0)),
        pl.BlockSpec((1, 4, D), lambda i, f: (layer, 0, 0)),
        pl.BlockSpec((1, D, tf), lambda i, f: (layer, 0, f)),
        pl.BlockSpec((1, D, tf), lambda i, f: (layer, 0, f)),
        pl.BlockSpec((1, tf, D), lambda i, f: (layer, f, 0)),
    ]
    args = [S, mods, norm_g, w1, w3, w2]
    out_specs = [pl.BlockSpec((tm, D), lambda i, f: (i, 0))]
    out_shape = [jax.ShapeDtypeStruct(S.shape, F32)]
    side = 0
    if adaln is not None:
        cvec, ada_w, ada_b = adaln
        n6 = 6 * D
        per_layer = n6 // SIDE_TN
        side = (DEPTH - 1) * per_layer
        assert side <= (S.shape[0] // tm) * nf

        def blk(i, f):
            t = jnp.minimum(i * nf + f, side - 1)
            return t // per_layer, t % per_layer

        in_specs += [
            pl.BlockSpec((8, D), lambda i, f: (0, 0)),
            pl.BlockSpec((1, D, SIDE_TN), lambda i, f: (1 + blk(i, f)[0], 0, blk(i, f)[1])),
            pl.BlockSpec((1, 1, SIDE_TN), lambda i, f: (1 + blk(i, f)[0], 0, blk(i, f)[1])),
        ]
        args += [cvec, ada_w, ada_b.reshape(DEPTH, 1, n6)]
        out_specs.append(pl.BlockSpec((1, 8, SIDE_TN), lambda i, f: (blk(i, f)[0], 0, blk(i, f)[1])))
        out_shape.append(jax.ShapeDtypeStruct((DEPTH - 1, 8, n6), F32))
    res = pl.pallas_call(
        functools.partial(_ffn_body, nf, side),
        grid=(S.shape[0] // tm, nf),
        in_specs=in_specs,
        out_specs=out_specs,
        out_shape=out_shape,
        scratch_shapes=[pltpu.VMEM((tm, D), BF16), pltpu.VMEM((tm, 128), F32)],
        compiler_params=_cparams(("arbitrary" if side else "parallel", "arbitrary")),
        name="ffn",
    )(*args)
    return res if adaln is not None else res[0]


def _rope_tables():
    quarter = HD // 4
    inv_freq = 10000.0 ** (-jnp.arange(quarter, dtype=F32) / quarter)
    row = jnp.repeat(jnp.arange(L // 64), 64).astype(F32)
    col = jnp.tile(jnp.arange(64), L // 64).astype(F32)
    ang_r = row[:, None] * inv_freq
    ang_c = col[:, None] * inv_freq
    ang = jnp.concatenate([ang_r, ang_r, ang_c, ang_c], axis=-1)
    reps = SUB_N // HD
    cos, sin = jnp.tile(jnp.cos(ang), (1, reps)), jnp.tile(jnp.sin(ang), (1, reps))
    first = (jnp.arange(SUB_N) % 32) < 16
    return cos, jnp.stack([jnp.where(first, -sin, 0.0), jnp.where(first, 0.0, sin)])


def kernel(x, c, ctx, c_ctx, ada_w, ada_b, norm_g, ffn_w1, ffn_w3, ffn_w2,
           gla_w_in, gla_wa2, gla_ba, gla_onorm_g, gla_wo,
           attn_w_in, attn_sink, attn_wo,
           gmlp_w_in, gmlp_ln_g, gmlp_ln_b, gmlp_ws, gmlp_bs, gmlp_wo):
    S = (ctx.reshape(RC, D), x.reshape(RX, D))
    cvec = jnp.concatenate([c_ctx[None, :], c, jnp.zeros((3, D), F32)], axis=0)
    mods0 = _modulation(cvec, ada_w, ada_b, 1).reshape(1, 8, 6, D)
    mods_rest = None
    rowtile = lambda i: (i, 0)
    w1, w3, w2 = ffn_w1, ffn_w3, ffn_w2
    gla_w, gla_wo_b = jnp.swapaxes(gla_w_in, 1, 2), gla_wo.astype(BF16)
    hk, hv = GH * GDK, GH * GDV

    for i in range(DEPTH):
        last = i == DEPTH - 1
        kind, slot = i % 3, i // 3
        mods = (mods0, 0) if i == 0 else (mods_rest, i - 1)
        if kind == 0:
            p, a = _proj(S, mods, norm_g, i, gla_w, slot, 2 * hk + 2 * hv, w_t=True,
                         small_rows=2 * GRANK, name="gla_proj")
            wa2 = gla_wa2[slot]
            wa2p = jnp.zeros((2, 2 * GRANK, hk), F32)
            wa2p = wa2p.at[0, :GRANK].set(wa2[0]).at[1, GRANK:].set(wa2[1]).astype(BF16)
            amat, masks = _gla_constants()
            o_f, o_b = _gla_scan(p, a, wa2p, gla_ba, slot, amat, masks, ctx_out=not last)
            ins = [o_f, o_b, p, gla_onorm_g.reshape(-1, 1, hv)]
            in_specs = [((TM_OUT, hv), rowtile), ((TM_OUT, hv), rowtile),
                        ((TM_OUT, hv), lambda i: (i, 2)),
                        ((1, 1, hv), functools.partial(lambda i, s: (s, 0, 0), s=slot))]
            S = _out_proj(_gla_prologue, ins, in_specs, gla_wo_b, slot, S, mods, norm_g, i,
                          skip_ctx=last, name="gla_out")
        elif kind == 1:
            cos, sin = _rope_tables()
            p = _proj(S, mods, norm_g, i, attn_w_in, slot, NQ + 2 * NKVD, kind="rope",
                      cos=cos, sin=sin, name="swa_proj")
            o = _swa_attention(p, attn_sink[slot:slot + 1])
            S = _out_proj(_swa_prologue, [o], [((TM_OUT, NQ), rowtile)], attn_wo.astype(BF16), slot,
                          S, mods, norm_g, i, skip_ctx=last, name="swa_out")
        else:
            p = _proj(S, mods, norm_g, i, gmlp_w_in, slot, 2 * D, kind="gelu",
                      name="gmlp_proj")
            bsb = jnp.repeat(gmlp_bs[slot].T, 128, axis=1)
            sel = functools.partial(lambda i, s: (s, 0, 0), s=slot)
            ins = [p, p, gmlp_ln_g.reshape(-1, 1, D), gmlp_ln_b.reshape(-1, 1, D),
                   gmlp_ws.astype(BF16), bsb]
            in_specs = [((TM_OUT, D), rowtile), ((TM_OUT, D), lambda i: (i, 1)),
                        ((1, 1, D), sel), ((1, 1, D), sel),
                        ((1, MG, MC, MC), functools.partial(lambda i, s: (s, 0, 0, 0), s=slot)),
                        ((MC, D), lambda i: (0, 0))]
            S = _out_proj(_gmlp_prologue, ins, in_specs, gmlp_wo.astype(BF16), slot, S, mods, norm_g, i,
                          skip_ctx=last, scratch=[pltpu.VMEM((TM_OUT, D), BF16)], name="gmlp_out")
        if i == 0:
            S, mods_rest = _ffn(S, mods, norm_g, w1, w3, w2, i, adaln=(cvec, ada_w, ada_b))
            mods_rest = mods_rest.reshape(DEPTH - 1, 8, 6, D)
        else:
            S = _ffn(S, mods, norm_g, w1, w3, w2, i)
    return S.reshape(B, L, D)
```

```python
import functools

import numpy as np
import jax
import jax.numpy as jnp
from jax import lax
from jax.experimental import pallas as pl
from jax.experimental.pallas import tpu as pltpu

F32 = jnp.float32
BF16 = jnp.bfloat16

D = 2048
B = 4
L = 2048
LC = 256
DEPTH = 4
FF = 5632
EPS = 1e-6
NEG_INF = -1e30
LOG2E = 1.4426950408889634

RC = B * LC
RX = B * L
R = RC + RX

GH = 4
GDK = 256
GDV = 512
GRANK = 16
GCH = 128
GN_CH = (LC + L) // GCH
GC_CH = LC // GCH
G_LEVELS = 7
G_SMALL = 3
G_ROWS = (1 + G_SMALL) * GCH

HD = 64
NH = 32
NKV = 4
GRP = 8
WIN = 128
QB = 128
NQ = NH * HD
NKVD = NKV * HD

MC = 128
MG = 16

TM_PROJ = 1024
TN_PROJ = (1280, 1024, 512)
TM_OUT = 512
TM_FFN = 1024
TF_FFN = 256

VMEM_LIMIT = 56 * 1024 * 1024


def _cparams(sem, vmem=VMEM_LIMIT):
    return pltpu.CompilerParams(dimension_semantics=sem, vmem_limit_bytes=vmem)


def _mod_row(i, tm):
    nct = RC // tm
    return jnp.where(i < nct, 0, 1 + (i - nct) // (L // tm))


def _norm_mod(x, ng, sc, sh):
    ms = jnp.mean(x * x, axis=-1, keepdims=True)
    return (x * lax.rsqrt(ms + EPS) * ng) * (1.0 + sc) + sh


def _rms(y, g):
    ms = jnp.mean(y * y, axis=-1, keepdims=True)
    return y * lax.rsqrt(ms + EPS) * g


ROW_CHUNK = 32


def _for_row_chunks(n_rows, fn, unroll=4):
    def body(r, carry):
        fn(pl.ds(pl.multiple_of(r * ROW_CHUNK, ROW_CHUNK), ROW_CHUNK))
        return carry
    lax.fori_loop(0, n_rows // ROW_CHUNK, body, 0, unroll=unroll)


def _row_rsqrt(load, rs_scr):
    def body(rows):
        x = load(rows)
        ms = jnp.mean(x * x, axis=-1, keepdims=True)
        rs_scr[rows, :] = jnp.broadcast_to(lax.rsqrt(ms + EPS), (ROW_CHUNK, 128))
    _for_row_chunks(rs_scr.shape[0], body, unroll=8)


def _lanes(rs, width):
    return jnp.concatenate([rs] * (width // 128), axis=1)


def _silu(a):
    return a * jax.nn.sigmoid(a)


def _dot(a, b):
    return jnp.dot(a, b, preferred_element_type=F32)


def _dot_nt(a, b):
    return lax.dot_general(a, b, (((1,), (1,)), ((), ())), preferred_element_type=F32)


def _dot_tn(a, b):
    return lax.dot_general(a, b, (((0,), (0,)), ((), ())), preferred_element_type=F32)


def _mod_body(c_ref, w_ref, b_ref, o_ref):
    s = _silu(c_ref[...]).astype(BF16)
    o_ref[0] = _dot(s, w_ref[0].astype(BF16)) + b_ref[0]


def _modulation(cvec, ada_w, ada_b, n_layers):
    tn = 1024
    n6 = 6 * D
    return pl.pallas_call(
        _mod_body,
        grid=(n_layers, n6 // tn),
        in_specs=[
            pl.BlockSpec((8, D), lambda l, n: (0, 0)),
            pl.BlockSpec((1, D, tn), lambda l, n: (l, 0, n)),
            pl.BlockSpec((1, 1, tn), lambda l, n: (l, 0, n)),
        ],
        out_specs=pl.BlockSpec((1, 8, tn), lambda l, n: (l, 0, n)),
        out_shape=jax.ShapeDtypeStruct((n_layers, 8, n6), F32),
        compiler_params=_cparams(("parallel", "parallel")),
        name="adaln_mod",
    )(cvec, ada_w, ada_b.reshape(DEPTH, 1, n6))


SUB_N = 256
SUB_M = 256


def _two_source_rows(is_ctx, c_ref, x_ref):
    return lambda rows: jnp.where(is_ctx, c_ref[rows, :], x_ref[rows, :])


def _proj_body(kind, small_rows, w_t, dual, ctx_cols, tm, tn, *refs):
    rest = list(refs)
    if dual:
        load_x = _two_source_rows(pl.program_id(0) < RC // tm, rest.pop(0), rest.pop(0))
    else:
        x_ref = rest.pop(0)
        load_x = lambda rows: x_ref[rows, :]
    mod_ref, ng_ref, w_ref = rest.pop(0), rest.pop(0), rest.pop(0)
    if kind == "rope":
        cos_ref, sin_ref = rest.pop(0), rest.pop(0)
    if small_rows:
        ws_ref = rest.pop(0)
    o_ref = rest.pop(0)
    if small_rows:
        os_ref = rest.pop(0)
    h_scr, rs_scr = rest.pop(0), rest.pop(0)
    i = pl.program_id(0)
    n = pl.program_id(1)

    @pl.when(n == 0)
    def _():
        _row_rsqrt(load_x, rs_scr)
        gsc = ng_ref[0, 0:1, :] * (1.0 + mod_ref[0, 0, 1:2, :])

        def pro(rows):
            h = load_x(rows) * _lanes(rs_scr[rows, :], D) * gsc + mod_ref[0, 0, 0:1, :]
            h_scr[rows, :] = h.astype(BF16)
        _for_row_chunks(tm, pro)
        if small_rows:
            os_ref[...] = _dot_nt(h_scr[...], ws_ref[0].astype(BF16)).astype(os_ref.dtype)

    if ctx_cols is not None:
        lo, hi = ctx_cols
        needed = (i >= RC // tm) | (((n + 1) * tn > lo) & (n * tn < hi))
        pl.when(needed)(lambda: _proj_columns(kind, w_t, tm, tn, i, n, h_scr, w_ref, o_ref, None, None))

        @pl.when(jnp.logical_not(needed))
        def _():
            o_ref[...] = jnp.zeros_like(o_ref)
    else:
        _proj_columns(kind, w_t, tm, tn, i, n, h_scr, w_ref, o_ref,
                      cos_ref if kind == "rope" else None, sin_ref if kind == "rope" else None)


def _proj_columns(kind, w_t, tm, tn, i, n, h_scr, w_ref, o_ref, cos_ref, sin_ref):
    sub_m = SUB_M if kind == "rope" else tm
    for c, r in [(c, r) for c in range(tn // SUB_N) for r in range(tm // sub_m)]:
        cs = slice(c * SUB_N, (c + 1) * SUB_N)
        rs = slice(r * sub_m, (r + 1) * sub_m)
        hb = h_scr[rs, :]
        if w_t:
            acc = _dot_nt(hb, w_ref[0, cs, :].astype(BF16))
        else:
            acc = _dot(hb, w_ref[0, :, cs].astype(BF16))
        if kind == "none":
            o_ref[rs, cs] = acc.astype(o_ref.dtype)
        elif kind == "gelu":
            o_ref[rs, cs] = jax.nn.gelu(acc, approximate=True).astype(o_ref.dtype)
        else:
            piece = n * (tn // SUB_N) + c
            scale = jnp.where(piece < NQ // SUB_N, HD ** -0.5, 1.0).astype(F32)
            rotate = jnp.logical_and(i >= RC // tm, piece < (NQ + NKVD) // SUB_N)

            roped = (acc * cos_ref[rs, :] + pltpu.roll(acc, SUB_N - 16, 1) * sin_ref[0, rs, :]
                     + pltpu.roll(acc, 16, 1) * sin_ref[1, rs, :])
            o_ref[rs, cs] = (jnp.where(rotate, roped, acc) * scale).astype(o_ref.dtype)


def _proj(S, mods, norm_g, layer, w, slot, n_cols, *, kind="none", w_t=False, small_rows=0,
          cos=None, sin=None, ctx_cols=None, name="proj"):
    tm = TM_PROJ
    tn = max(t for t in TN_PROJ if n_cols % t == 0)
    nct = RC // tm
    mods, mlayer = mods
    once = pl.Buffered(1)
    dual = isinstance(S, tuple)
    if dual:
        row_specs = [pl.BlockSpec((tm, D), lambda i, n: (jnp.minimum(i, nct - 1), 0), pipeline_mode=once),
                     pl.BlockSpec((tm, D), lambda i, n: (jnp.maximum(i - nct, 0), 0))]
    else:
        row_specs = [pl.BlockSpec((tm, D), lambda i, n: (i, 0))]
    in_specs = row_specs + [
        pl.BlockSpec((1, 1, 6, D), lambda i, n: (mlayer, _mod_row(i, tm), 0, 0)),
        pl.BlockSpec((1, 4, D), lambda i, n: (layer, 0, 0)),
        (pl.BlockSpec((1, tn, D), lambda i, n: (slot, n, 0)) if w_t
         else pl.BlockSpec((1, D, tn), lambda i, n: (slot, 0, n))),
    ]
    args = (list(S) if dual else [S]) + [mods, norm_g, w]
    if kind == "rope":
        pos = lambda i, n: jnp.where(i < nct, 0, (i - nct) % (L // tm))
        in_specs += [pl.BlockSpec((tm, SUB_N), lambda i, n: (pos(i, n), 0)),
                     pl.BlockSpec((2, tm, SUB_N), lambda i, n: (0, pos(i, n), 0))]
        args += [cos, sin]
    out_specs = [pl.BlockSpec((tm, tn), lambda i, n: (i, n))]
    out_shape = [jax.ShapeDtypeStruct((R, n_cols), BF16)]
    if small_rows:
        assert w_t and n_cols % small_rows == 0
        in_specs.append(pl.BlockSpec((1, small_rows, D), lambda i, n: (slot, n_cols // small_rows, 0)))
        args.append(w)
        out_specs.append(pl.BlockSpec((tm, small_rows), lambda i, n: (i, 0)))
        out_shape.append(jax.ShapeDtypeStruct((R, small_rows), BF16))
    res = pl.pallas_call(
        functools.partial(_proj_body, kind, small_rows, w_t, dual, ctx_cols, tm, tn),
        grid=(R // tm, n_cols // tn),
        in_specs=in_specs,
        out_specs=out_specs,
        out_shape=out_shape,
        scratch_shapes=[pltpu.VMEM((tm, D), BF16), pltpu.VMEM((tm, 128), F32)],
        compiler_params=_cparams(("parallel", "arbitrary")),
        name=name,
    )(*args)
    return res if small_rows else res[0]


def _gla_constants():
    c = GCH
    r = np.arange(c)[:, None]
    t = np.arange(c)[None, :]
    blocks = [(t <= r)]
    masks = []
    for lv in range(G_LEVELS):
        s = c >> (lv + 1)
        m = r // s
        odd = (m % 2) == 1
        if s < 8:
            a_odd = (t > s * m) & (t <= r)
            a_even = (t > r) & (t <= s * (m + 1))
            blocks.append(np.where(odd, a_odd, a_even))
        masks.append(odd & ((t // s) == m - 1))
    masks.append(r == t)
    a_f = np.concatenate(blocks, axis=0).astype(np.float32)
    m_f = np.stack(masks).astype(np.float32)
    a_b = np.concatenate([blk[::-1, ::-1] for blk in blocks], axis=0).astype(np.float32)
    m_b = m_f[:, ::-1, ::-1]
    amat = np.stack([a_f, a_b])
    amat = np.concatenate([amat, amat], axis=2)
    return jnp.asarray(amat, BF16), jnp.asarray(np.stack([m_f, m_b]), F32)


def _gla_body(ctx_out, qf, kf, vf, af, qb, kb, vb, ab, wa_ref, ba_ref, a_ref, m_ref, of_ref, ob_ref, s_scr):
    c = pl.program_id(1)

    @pl.when(c == 0)
    def _():
        s_scr[...] = jnp.zeros_like(s_scr)

    row = lax.broadcasted_iota(jnp.int32, (GCH, GDK), 0)
    odd_rows = {s: ((row // s) % 2) == 1 for s in (4, 2, 1)}
    dirs = ((qf, kf, vf, af, of_ref, GCH - 1), (qb, kb, vb, ab, ob_ref, 0))

    def process(with_out):
        for d, (q_ref, k_ref, v_ref, ar_ref, o_ref, last) in enumerate(dirs):
            araw = _dot(ar_ref[...], wa_ref[d]) + ba_ref[d:d + 1, :]
            g = (jnp.minimum(araw, 0.0) - jnp.log1p(jnp.exp(-jnp.abs(araw)))) * (LOG2E / 16.0)
            g1 = g.astype(BF16)
            g2 = (g - g1.astype(F32)).astype(BF16)
            amat = a_ref[d] if with_out else a_ref[d, 0:GCH, :]
            ex_mm = _dot(amat, jnp.concatenate([g1, g2], axis=0))

            def level_exponent(bcum, ex_h, lv):
                s = GCH >> (lv + 1)
                if s < 8:
                    n_big = G_LEVELS - G_SMALL
                    return ex_h[(1 + lv - n_big) * GCH:(2 + lv - n_big) * GCH]
                parts = []
                for p in range(GCH // (2 * s)):
                    lo = 2 * p * s
                    ev, od = bcum[lo:lo + s], bcum[lo + s:lo + 2 * s]
                    if d == 0:
                        ref = bcum[lo + s:lo + s + 1]
                        parts += [ref - ev, od - ref]
                    else:
                        ref = bcum[lo + s - 1:lo + s]
                        parts += [ev - ref, ref - od]
                return jnp.concatenate(parts, axis=0)

            def zsel(q, k, lv):
                s = GCH >> (lv + 1)
                if s >= 8:
                    parts = [(q if (m % 2 == 1) == (d == 0) else k)[m * s:(m + 1) * s]
                             for m in range(GCH // s)]
                    return jnp.concatenate(parts, axis=0)
                return jnp.where(odd_rows[s], q, k) if d == 0 else jnp.where(odd_rows[s], k, q)

            if not with_out:
                o_ref[...] = jnp.zeros_like(o_ref)
            for h in range(GH):
                sl = slice(h * GDK, (h + 1) * GDK)
                vs = slice(h * GDV, (h + 1) * GDV)
                k = k_ref[:, sl].astype(F32)
                v = v_ref[:, vs]
                ex_h = ex_mm[:, sl]
                bcum = ex_h[0:GCH]
                e_cum = jnp.exp2(bcum)
                e_rem = jnp.exp2((bcum[GCH - 1:GCH] if d == 0 else bcum[0:1]) - bcum)
                st = s_scr[d, h]
                if with_out:
                    q = q_ref[:, sl].astype(F32) * (GDK ** -0.5)
                    o = _dot_nt((q * e_cum).astype(BF16), st.astype(BF16))
                    att = m_ref[d, G_LEVELS] * _dot_nt(q.astype(BF16), k.astype(BF16))
                    for lv in range(G_LEVELS):
                        z = (zsel(q, k, lv) * jnp.exp2(level_exponent(bcum, ex_h, lv))).astype(BF16)
                        att = att + m_ref[d, lv] * _dot_nt(z, z)
                    o = o + _dot(att.astype(BF16), v)
                    o_ref[:, vs] = o.astype(o_ref.dtype)
                s_scr[d, h] = st * e_cum[last:last + 1, :] + _dot_tn(v, (k * e_rem).astype(BF16))

    if ctx_out:
        process(True)
    else:
        pl.when(c >= GC_CH)(lambda: process(True))
        pl.when(c < GC_CH)(lambda: process(False))


def _gla_scan(p, a, wa2p, ba, slot, amat, masks, ctx_out):
    def rowblk(b, m):
        return jnp.where(m < GC_CH, b * GC_CH + m, RC // GCH + b * (L // GCH) + (m - GC_CH))

    def bwd(c):
        return jnp.where(c < GC_CH, GC_CH - 1 - c, GN_CH + GC_CH - 1 - c)

    hk = GH * GDK
    hv = GH * GDV

    def specs(order):
        return [
            pl.BlockSpec((GCH, hk), lambda b, c: (rowblk(b, order(c)), 0)),
            pl.BlockSpec((GCH, hk), lambda b, c: (rowblk(b, order(c)), 1)),
            pl.BlockSpec((GCH, hv), lambda b, c: (rowblk(b, order(c)), 1)),
            pl.BlockSpec((GCH, 2 * GRANK), lambda b, c: (rowblk(b, order(c)), 0)),
        ]

    in_specs = specs(lambda c: c) + specs(bwd) + [
        pl.BlockSpec((2, 2 * GRANK, hk), lambda b, c: (0, 0, 0)),
        pl.BlockSpec((1, 2, hk), lambda b, c: (slot, 0, 0)),
        pl.BlockSpec((2, G_ROWS, 2 * GCH), lambda b, c: (0, 0, 0)),
        pl.BlockSpec((2, G_LEVELS + 1, GCH, GCH), lambda b, c: (0, 0, 0, 0)),
    ]
    out_specs = [
        pl.BlockSpec((GCH, hv), lambda b, c: (rowblk(b, c), 0)),
        pl.BlockSpec((GCH, hv), lambda b, c: (rowblk(b, bwd(c)), 0)),
    ]

    def body(qf, kf, vf, af, qb, kb, vb, ab, wa_ref, ba_ref, a_ref, m_ref, of_ref, ob_ref, s_scr):
        _gla_body(ctx_out, qf, kf, vf, af, qb, kb, vb, ab, wa_ref, ba_ref.at[0], a_ref, m_ref, of_ref, ob_ref, s_scr)

    return pl.pallas_call(
        body,
        grid=(B, GN_CH),
        in_specs=in_specs,
        out_specs=out_specs,
        out_shape=[jax.ShapeDtypeStruct((R, hv), BF16)] * 2,
        scratch_shapes=[pltpu.VMEM((2, GH, GDV, GDK), F32)],
        compiler_params=_cparams(("parallel", "arbitrary")),
        name="gla_scan",
    )(p, p, p, a, p, p, p, a, wa2p, ba, amat, masks)


def _swa_body(sink_ref, q_ref, kc_ref, vc_ref, k0, k1, k2, v0, v1, v2, o_ref):
    j = pl.program_id(1)
    is_ctx = j < LC // QB
    jl = j - LC // QB
    lo = jnp.where(is_ctx, 0, jnp.where(jl >= 1, 0, QB))
    hi = jnp.where(is_ctx, 0, jnp.where(jl <= L // QB - 2, 3 * QB, 2 * QB))
    nk = LC + 3 * QB
    rr = lax.broadcasted_iota(jnp.int32, (QB, nk), 0)
    tt = lax.broadcasted_iota(jnp.int32, (QB, nk), 1) - LC
    dlt = tt - rr
    valid = (tt < 0) | ((dlt >= 0) & (dlt <= 2 * WIN) & (tt >= lo) & (tt < hi))
    bias = jnp.where(valid, 0.0, NEG_INF)

    kcat = jnp.concatenate([kc_ref[...], k0[...], k1[...], k2[...]], axis=0).astype(F32)
    vcat = jnp.concatenate([vc_ref[...], v0[...], v1[...], v2[...]], axis=0).astype(F32)
    lane_q = lax.broadcasted_iota(jnp.int32, (QB, 128), 1)

    def dup(xp, half):
        low = lax.broadcasted_iota(jnp.int32, xp.shape, 1) < HD
        rolled = pltpu.roll(xp, HD, 1)
        out = jnp.where(low, xp, rolled) if half == 0 else jnp.where(low, rolled, xp)
        return out.astype(BF16)

    for kh in range(NKV):
        ps = slice((kh // 2) * 128, (kh // 2 + 1) * 128)
        kd, vd = dup(kcat[:, ps], kh % 2), dup(vcat[:, ps], kh % 2)
        qs = []
        for g in range(GRP):
            h = kh * GRP + g
            blk = q_ref[:, (h // 2) * 128:(h // 2 + 1) * 128].astype(F32)
            keep = (lane_q < HD) if h % 2 == 0 else (lane_q >= HD)
            qs.append(jnp.where(keep, blk, 0.0).astype(BF16))
        qst = jnp.concatenate(qs, axis=0)
        logits = _dot_nt(qst, kd)
        es, invs = [], []
        for g in range(GRP):
            lg = logits[g * QB:(g + 1) * QB] + bias
            s = sink_ref[0, kh * GRP + g]
            m = jnp.maximum(jnp.max(lg, axis=-1, keepdims=True), s)
            e = jnp.exp(lg - m)
            den = jnp.sum(e, axis=-1, keepdims=True) + jnp.exp(s - m)
            es.append(e.astype(BF16))
            invs.append(1.0 / den)
        res = _dot(jnp.concatenate(es, axis=0), vd) * jnp.concatenate(invs, axis=0)
        for m2 in range(GRP // 2):
            even = res[(2 * m2) * QB:(2 * m2 + 1) * QB]
            odd = res[(2 * m2 + 1) * QB:(2 * m2 + 2) * QB]
            col = (kh * (GRP // 2) + m2) * 128
            o_ref[:, col:col + 128] = jnp.where(lane_q < HD, even, odd).astype(o_ref.dtype)


def _swa_attention(p, sink):
    ncb = LC // QB
    nlb = L // QB
    lat0 = RC // QB
    kcol = NQ // NKVD

    def qrow(b, j):
        return jnp.where(j < ncb, b * ncb + j, lat0 + b * nlb + (j - ncb))

    def wrow(off):
        def f(b, j):
            jl = jnp.clip(j - ncb + off, 0, nlb - 1)
            return lat0 + b * nlb + jl
        return f

    in_specs = [
        pl.BlockSpec(memory_space=pltpu.SMEM),
        pl.BlockSpec((QB, NQ), lambda b, j: (qrow(b, j), 0)),
        pl.BlockSpec((LC, NKVD), lambda b, j: (b, kcol)),
        pl.BlockSpec((LC, NKVD), lambda b, j: (b, kcol + 1)),
    ]
    for col in (kcol, kcol + 1):
        for off in (-1, 0, 1):
            in_specs.append(pl.BlockSpec((QB, NKVD), functools.partial(
                lambda b, j, f, cc: (f(b, j), cc), f=wrow(off), cc=col)))
    return pl.pallas_call(
        _swa_body,
        grid=(B, ncb + nlb),
        in_specs=in_specs,
        out_specs=pl.BlockSpec((QB, NQ), lambda b, j: (qrow(b, j), 0)),
        out_shape=jax.ShapeDtypeStruct((R, NQ), BF16),
        compiler_params=_cparams(("parallel", "parallel")),
        name="swa_attn",
    )(sink, p, p, p, p, p, p, p, p, p)


def _gla_prologue(rows, of_ref, ob_ref, og_ref, on_ref):
    o = of_ref[rows, :].astype(F32) + ob_ref[rows, :].astype(F32)
    parts = [_rms(o[:, h * GDV:(h + 1) * GDV], on_ref[0, :, h * GDV:(h + 1) * GDV]) for h in range(GH)]
    return (jnp.concatenate(parts, axis=-1) * _silu(og_ref[rows, :].astype(F32))).astype(BF16)


def _swa_prologue(rows, o_ref):
    return o_ref[rows, :]


def _gmlp_prologue(rows, u_ref, v_ref, lng_ref, lnb_ref, ws_ref, bsb_ref, lhs_scr):
    v = v_ref[rows, :].astype(F32)
    mu = jnp.mean(v, axis=-1, keepdims=True)
    var = jnp.mean(jnp.square(v - mu), axis=-1, keepdims=True)
    vn = ((v - mu) * lax.rsqrt(var + EPS) * lng_ref[0] + lnb_ref[0]).astype(BF16)
    for ch in range(v.shape[0] // MC):
        rs = slice(ch * MC, (ch + 1) * MC)
        rg = slice(rows.start + ch * MC, rows.start + (ch + 1) * MC)
        for g in range(MG):
            cs = slice(g * 128, (g + 1) * 128)
            mixed = _dot(ws_ref[0, g], vn[rs, cs]) + bsb_ref[:, cs]
            lhs_scr[rg, cs] = (u_ref[rg, cs].astype(F32) * mixed).astype(BF16)
    return lhs_scr[rows, :]


OUT_SPLIT = 2


def _out_body(prologue, n_in, dual, *refs):
    in_refs = refs[:n_in]
    rest = list(refs[n_in:])
    wo_ref = rest.pop(0)
    if dual:
        load_x = _two_source_rows(pl.program_id(0) < RC // TM_OUT, rest.pop(0), rest.pop(0))
    else:
        x_ref = rest.pop(0)
        load_x = lambda rows: x_ref[rows, :]
    mod_ref, ng_ref, o_ref = rest[:3]
    piece = o_ref.shape[0] // OUT_SPLIT
    for r in range(OUT_SPLIT):
        rows = slice(r * piece, (r + 1) * piece)
        y = _dot(prologue(rows, *in_refs, *rest[3:]), wo_ref[0])
        o_ref[rows, :] = load_x(rows) + mod_ref[0, 0, 2:3, :] * _rms(y, ng_ref[0, 1:2, :])


def _out_proj(prologue, ins, in_specs, wo, slot, S, mods, norm_g, layer, *, skip_ctx, scratch=(),
              name="out_proj"):
    tm = TM_OUT
    mods, mlayer = mods
    off = RC // tm if skip_ctx else 0
    rows = RX if skip_ctx else R
    specs = [pl.BlockSpec(bs, functools.partial(lambda i, f: f(i + off), f=f)) for bs, f in in_specs]
    dual = isinstance(S, tuple)
    nct = RC // tm
    if dual:
        assert not skip_ctx
        row_specs = [pl.BlockSpec((tm, D), lambda i: (jnp.minimum(i, nct - 1), 0), pipeline_mode=pl.Buffered(1)),
                     pl.BlockSpec((tm, D), lambda i: (jnp.maximum(i - nct, 0), 0))]
    else:
        row_specs = [pl.BlockSpec((tm, D), lambda i: (i + off, 0))]
    specs += [pl.BlockSpec((1, D, D), lambda i: (slot, 0, 0), pipeline_mode=pl.Buffered(1))] + row_specs + [
        pl.BlockSpec((1, 1, 6, D), lambda i: (mlayer, _mod_row(i + off, tm), 0, 0)),
        pl.BlockSpec((1, 4, D), lambda i: (layer, 0, 0)),
    ]
    return pl.pallas_call(
        functools.partial(_out_body, prologue, len(ins), dual),
        grid=(rows // tm,),
        in_specs=specs,
        out_specs=pl.BlockSpec((tm, D), lambda i: (i, 0)),
        out_shape=jax.ShapeDtypeStruct((rows, D), F32),
        scratch_shapes=list(scratch),
        compiler_params=_cparams(("parallel",)),
        name=name,
    )(*ins, wo, *(S if dual else (S,)), mods, norm_g)


SIDE_TN = 256


def _ffn_body(nf, side, x_ref, mod_ref, ng_ref, w1_ref, w3_ref, w2_ref, *rest):
    f = pl.program_id(1)
    if side:
        c_ref, aw_ref, ab_ref, o_ref, mo_ref, h_scr, rs_scr = rest

        def side_job():
            mo_ref[0] = _dot(_silu(c_ref[...]).astype(BF16), aw_ref[0].astype(BF16)) + ab_ref[0]
    else:
        o_ref, h_scr, rs_scr = rest
        side_job = lambda: None

    @pl.when(f == 0)
    def _():
        _row_rsqrt(lambda rows: x_ref[rows, :], rs_scr)
        gsc = ng_ref[0, 2:3, :] * (1.0 + mod_ref[0, 0, 4:5, :])

        def pro(rows):
            h = x_ref[rows, :] * _lanes(rs_scr[rows, :], D) * gsc + mod_ref[0, 0, 3:4, :]
            h_scr[rows, :] = h.astype(BF16)
        _for_row_chunks(x_ref.shape[0], pro)

    def partial_sum():
        hb = h_scr[...]
        a = _dot(hb, w1_ref[0].astype(BF16))
        b = _dot(hb, w3_ref[0].astype(BF16))
        return _dot((_silu(a) * b).astype(BF16), w2_ref[0].astype(BF16))

    @pl.when(f == 0)
    def _():
        side_job()
        o_ref[...] = partial_sum()

    @pl.when(f > 0)
    def _():
        side_job()
        o_ref[...] += partial_sum()

    @pl.when(f == nf - 1)
    def _():
        _row_rsqrt(lambda rows: o_ref[rows, :], rs_scr)
        gg = mod_ref[0, 0, 5:6, :] * ng_ref[0, 3:4, :]

        def epi(rows):
            o_ref[rows, :] = x_ref[rows, :] + o_ref[rows, :] * _lanes(rs_scr[rows, :], D) * gg
        _for_row_chunks(x_ref.shape[0], epi)


def _ffn(S, mods, norm_g, w1, w3, w2, layer, adaln=None):
    tm, tf = TM_FFN, TF_FFN
    nf = FF // tf
    off = (R - S.shape[0]) // tm
    mods, mlayer = mods
    in_specs = [
        pl.BlockSpec((tm, D), lambda i, f: (i, 0)),
        pl.BlockSpec((1, 1, 6, D), lambda i, f: (mlayer, _mod_row(i + off, tm), 0, 0)),
        pl.BlockSpec((1, 4, D), lambda i, f: (layer, 0, 0)),
        pl.BlockSpec((1, D, tf), lambda i, f: (layer, 0, f)),
        pl.BlockSpec((1, D, tf), lambda i, f: (layer, 0, f)),
        pl.BlockSpec((1, tf, D), lambda i, f: (layer, f, 0)),
    ]
    args = [S, mods, norm_g, w1, w3, w2]
    out_specs = [pl.BlockSpec((tm, D), lambda i, f: (i, 0))]
    out_shape = [jax.ShapeDtypeStruct(S.shape, F32)]
    side = 0
    if adaln is not None:
        cvec, ada_w, ada_b = adaln
        n6 = 6 * D
        per_layer = n6 // SIDE_TN
        side = (DEPTH - 1) * per_layer
        assert side <= (S.shape[0] // tm) * nf

        def blk(i, f):
            t = jnp.minimum(i * nf + f, side - 1)
            return t // per_layer, t % per_layer

        in_specs += [
            pl.BlockSpec((8, D), lambda i, f: (0, 0)),
            pl.BlockSpec((1, D, SIDE_TN), lambda i, f: (1 + blk(i, f)[0], 0, blk(i, f)[1])),
            pl.BlockSpec((1, 1, SIDE_TN), lambda i, f: (1 + blk(i, f)[0], 0, blk(i, f)[1])),
        ]
        args += [cvec, ada_w, ada_b.reshape(DEPTH, 1, n6)]
        out_specs.append(pl.BlockSpec((1, 8, SIDE_TN), lambda i, f: (blk(i, f)[0], 0, blk(i, f)[1])))
        out_shape.append(jax.ShapeDtypeStruct((DEPTH - 1, 8, n6), F32))
    res = pl.pallas_call(
        functools.partial(_ffn_body, nf, side),
        grid=(S.shape[0] // tm, nf),
        in_specs=in_specs,
        out_specs=out_specs,
        out_shape=out_shape,
        scratch_shapes=[pltpu.VMEM((tm, D), BF16), pltpu.VMEM((tm, 128), F32)],
        compiler_params=_cparams(("arbitrary" if side else "parallel", "arbitrary")),
        name="ffn",
    )(*args)
    return res if adaln is not None else res[0]


def _rope_tables():
    quarter = HD // 4
    inv_freq = 10000.0 ** (-jnp.arange(quarter, dtype=F32) / quarter)
    row = jnp.repeat(jnp.arange(L // 64), 64).astype(F32)
    col = jnp.tile(jnp.arange(64), L // 64).astype(F32)
    ang_r = row[:, None] * inv_freq
    ang_c = col[:, None] * inv_freq
    ang = jnp.concatenate([ang_r, ang_r, ang_c, ang_c], axis=-1)
    reps = SUB_N // HD
    cos, sin = jnp.tile(jnp.cos(ang), (1, reps)), jnp.tile(jnp.sin(ang), (1, reps))
    first = (jnp.arange(SUB_N) % 32) < 16
    return cos, jnp.stack([jnp.where(first, -sin, 0.0), jnp.where(first, 0.0, sin)])


def kernel(x, c, ctx, c_ctx, ada_w, ada_b, norm_g, ffn_w1, ffn_w3, ffn_w2,
           gla_w_in, gla_wa2, gla_ba, gla_onorm_g, gla_wo,
           attn_w_in, attn_sink, attn_wo,
           gmlp_w_in, gmlp_ln_g, gmlp_ln_b, gmlp_ws, gmlp_bs, gmlp_wo):
    S = (ctx.reshape(RC, D), x.reshape(RX, D))
    cvec = jnp.concatenate([c_ctx[None, :], c, jnp.zeros((3, D), F32)], axis=0)
    mods0 = _modulation(cvec, ada_w, ada_b, 1).reshape(1, 8, 6, D)
    mods_rest = None
    rowtile = lambda i: (i, 0)
    w1, w3, w2 = ffn_w1, ffn_w3, ffn_w2
    gla_w, gla_wo_b = jnp.swapaxes(gla_w_in, 1, 2), gla_wo.astype(BF16)
    hk, hv = GH * GDK, GH * GDV

    for i in range(DEPTH):
        last = i == DEPTH - 1
        kind, slot = i % 3, i // 3
        mods = (mods0, 0) if i == 0 else (mods_rest, i - 1)
        if kind == 0:
            p, a = _proj(S, mods, norm_g, i, gla_w, slot, 2 * hk + 2 * hv, w_t=True, small_rows=2 * GRANK,
                         ctx_cols=(hk, 2 * hk + hv) if last else None, name="gla_proj")
            wa2 = gla_wa2[slot]
            wa2p = jnp.zeros((2, 2 * GRANK, hk), F32)
            wa2p = wa2p.at[0, :GRANK].set(wa2[0]).at[1, GRANK:].set(wa2[1]).astype(BF16)
            amat, masks = _gla_constants()
            o_f, o_b = _gla_scan(p, a, wa2p, gla_ba, slot, amat, masks, ctx_out=not last)
            ins = [o_f, o_b, p, gla_onorm_g.reshape(-1, 1, hv)]
            in_specs = [((TM_OUT, hv), rowtile), ((TM_OUT, hv), rowtile),
                        ((TM_OUT, hv), lambda i: (i, 2)),
                        ((1, 1, hv), functools.partial(lambda i, s: (s, 0, 0), s=slot))]
            S = _out_proj(_gla_prologue, ins, in_specs, gla_wo_b, slot, S, mods, norm_g, i,
                          skip_ctx=last, name="gla_out")
        elif kind == 1:
            cos, sin = _rope_tables()
            p = _proj(S, mods, norm_g, i, attn_w_in, slot, NQ + 2 * NKVD, kind="rope",
                      cos=cos, sin=sin, name="swa_proj")
            o = _swa_attention(p, attn_sink[slot:slot + 1])
            S = _out_proj(_swa_prologue, [o], [((TM_OUT, NQ), rowtile)], attn_wo.astype(BF16), slot,
                          S, mods, norm_g, i, skip_ctx=last, name="swa_out")
        else:
            p = _proj(S, mods, norm_g, i, gmlp_w_in, slot, 2 * D, kind="gelu",
                      name="gmlp_proj")
            bsb = jnp.repeat(gmlp_bs[slot].T, 128, axis=1)
            sel = functools.partial(lambda i, s: (s, 0, 0), s=slot)
            ins = [p, p, gmlp_ln_g.reshape(-1, 1, D), gmlp_ln_b.reshape(-1, 1, D),
                   gmlp_ws.astype(BF16), bsb]
            in_specs = [((TM_OUT, D), rowtile), ((TM_OUT, D), lambda i: (i, 1)),
                        ((1, 1, D), sel), ((1, 1, D), sel),
                        ((1, MG, MC, MC), functools.partial(lambda i, s: (s, 0, 0, 0), s=slot)),
                        ((MC, D), lambda i: (0, 0))]
            S = _out_proj(_gmlp_prologue, ins, in_specs, gmlp_wo.astype(BF16), slot, S, mods, norm_g, i,
                          skip_ctx=last, scratch=[pltpu.VMEM((TM_OUT, D), BF16)], name="gmlp_out")
        if i == 0:
            S, mods_rest = _ffn(S, mods, norm_g, w1, w3, w2, i, adaln=(cvec, ada_w, ada_b))
            mods_rest = mods_rest.reshape(DEPTH - 1, 8, 6, D)
        else:
            S = _ffn(S, mods, norm_g, w1, w3, w2, i)
    return S.reshape(B, L, D)
```

```python
import functools

import numpy as np
import jax
import jax.numpy as jnp
from jax import lax
from jax.experimental import pallas as pl
from jax.experimental.pallas import tpu as pltpu

F32 = jnp.float32
BF16 = jnp.bfloat16

D = 2048
B = 4
L = 2048
LC = 256
DEPTH = 4
FF = 5632
EPS = 1e-6
NEG_INF = -1e30
LOG2E = 1.4426950408889634

RC = B * LC
RX = B * L
R = RC + RX

GH = 4
GDK = 256
GDV = 512
GRANK = 16
GCH = 128
GN_CH = (LC + L) // GCH
GC_CH = LC // GCH
G_LEVELS = 7
G_SMALL = 3
G_ROWS = (1 + G_SMALL) * GCH

HD = 64
NH = 32
NKV = 4
GRP = 8
WIN = 128
QB = 128
NQ = NH * HD
NKVD = NKV * HD

MC = 128
MG = 16

TM_PROJ = 1024
TN_PROJ = (1280, 1024, 512)
TM_OUT = 512
TM_FFN = 1024
TF_FFN = 256

VMEM_LIMIT = 56 * 1024 * 1024


def _cparams(sem, vmem=VMEM_LIMIT):
    return pltpu.CompilerParams(dimension_semantics=sem, vmem_limit_bytes=vmem)


def _mod_row(i, tm):
    nct = RC // tm
    return jnp.where(i < nct, 0, 1 + (i - nct) // (L // tm))


def _rms(y, g):
    ms = jnp.mean(y * y, axis=-1, keepdims=True)
    return y * lax.rsqrt(ms + EPS) * g


ROW_CHUNK = 32


def _for_row_chunks(n_rows, fn, unroll=4):
    def body(r, carry):
        fn(pl.ds(pl.multiple_of(r * ROW_CHUNK, ROW_CHUNK), ROW_CHUNK))
        return carry
    lax.fori_loop(0, n_rows // ROW_CHUNK, body, 0, unroll=unroll)


def _row_rsqrt(load, rs_scr):
    def body(rows):
        x = load(rows)
        ms = jnp.mean(x * x, axis=-1, keepdims=True)
        rs_scr[rows, :] = jnp.broadcast_to(lax.rsqrt(ms + EPS), (ROW_CHUNK, 128))
    _for_row_chunks(rs_scr.shape[0], body, unroll=8)


def _lanes(rs, width):
    return jnp.concatenate([rs] * (width // 128), axis=1)


def _silu(a):
    return a * jax.nn.sigmoid(a)


def _dot(a, b):
    return jnp.dot(a, b, preferred_element_type=F32)


def _dot_nt(a, b):
    return lax.dot_general(a, b, (((1,), (1,)), ((), ())), preferred_element_type=F32)


def _dot_tn(a, b):
    return lax.dot_general(a, b, (((0,), (0,)), ((), ())), preferred_element_type=F32)


def _mod_body(c_ref, w_ref, b_ref, o_ref):
    s = _silu(c_ref[...]).astype(BF16)
    o_ref[0] = _dot(s, w_ref[0].astype(BF16)) + b_ref[0]


def _modulation(cvec, ada_w, ada_b, n_layers):
    tn = 1024
    n6 = 6 * D
    return pl.pallas_call(
        _mod_body,
        grid=(n_layers, n6 // tn),
        in_specs=[
            pl.BlockSpec((8, D), lambda l, n: (0, 0)),
            pl.BlockSpec((1, D, tn), lambda l, n: (l, 0, n)),
            pl.BlockSpec((1, 1, tn), lambda l, n: (l, 0, n)),
        ],
        out_specs=pl.BlockSpec((1, 8, tn), lambda l, n: (l, 0, n)),
        out_shape=jax.ShapeDtypeStruct((n_layers, 8, n6), F32),
        compiler_params=_cparams(("parallel", "parallel")),
        name="adaln_mod",
    )(cvec, ada_w, ada_b.reshape(DEPTH, 1, n6))


SUB_N = 256
SUB_M = 256


def _with_row_source(dual, tile_rows, refs, fn):
    if not dual:
        x_ref = refs.pop(0)
        return lambda: fn(lambda rows: x_ref[rows, :])
    c_ref, x_ref = refs.pop(0), refs.pop(0)
    is_ctx = pl.program_id(0) < RC // tile_rows

    def run():
        pl.when(is_ctx)(lambda: fn(lambda rows: c_ref[rows, :]))
        pl.when(jnp.logical_not(is_ctx))(lambda: fn(lambda rows: x_ref[rows, :]))
    return run


def _proj_body(kind, small_rows, w_t, dual, ctx_cols, tm, tn, *refs):
    rest = list(refs)
    mod_ref, ng_ref = rest[2 if dual else 1], rest[3 if dual else 2]
    h_scr, rs_scr = rest[-2], rest[-1]

    def norm_rows(load_x):
        _row_rsqrt(load_x, rs_scr)
        gsc = ng_ref[0, 0:1, :] * (1.0 + mod_ref[0, 0, 1:2, :])

        def pro(rows):
            h = load_x(rows) * _lanes(rs_scr[rows, :], D) * gsc + mod_ref[0, 0, 0:1, :]
            h_scr[rows, :] = h.astype(BF16)
        _for_row_chunks(tm, pro)

    prologue = _with_row_source(dual, tm, rest, norm_rows)
    mod_ref, ng_ref, w_ref = rest.pop(0), rest.pop(0), rest.pop(0)
    if kind == "rope":
        cos_ref, sin_ref = rest.pop(0), rest.pop(0)
    if small_rows:
        ws_ref = rest.pop(0)
    o_ref = rest.pop(0)
    if small_rows:
        os_ref = rest.pop(0)
    h_scr, rs_scr = rest.pop(0), rest.pop(0)
    i = pl.program_id(0)
    n = pl.program_id(1)

    @pl.when(n == 0)
    def _():
        prologue()
        if small_rows:
            os_ref[...] = _dot_nt(h_scr[...], ws_ref[0].astype(BF16)).astype(os_ref.dtype)

    if ctx_cols is not None:
        lo, hi = ctx_cols
        needed = (i >= RC // tm) | (((n + 1) * tn > lo) & (n * tn < hi))
        pl.when(needed)(lambda: _proj_columns(kind, w_t, tm, tn, i, n, h_scr, w_ref, o_ref, None, None))

        @pl.when(jnp.logical_not(needed))
        def _():
            o_ref[...] = jnp.zeros_like(o_ref)
    else:
        _proj_columns(kind, w_t, tm, tn, i, n, h_scr, w_ref, o_ref,
                      cos_ref if kind == "rope" else None, sin_ref if kind == "rope" else None)


def _proj_columns(kind, w_t, tm, tn, i, n, h_scr, w_ref, o_ref, cos_ref, sin_ref):
    sub_m = SUB_M if kind == "rope" else tm
    for c, r in [(c, r) for c in range(tn // SUB_N) for r in range(tm // sub_m)]:
        cs = slice(c * SUB_N, (c + 1) * SUB_N)
        rs = slice(r * sub_m, (r + 1) * sub_m)
        hb = h_scr[rs, :]
        if w_t:
            acc = _dot_nt(hb, w_ref[0, cs, :].astype(BF16))
        else:
            acc = _dot(hb, w_ref[0, :, cs].astype(BF16))
        if kind == "none":
            o_ref[rs, cs] = acc.astype(o_ref.dtype)
        elif kind == "gelu":
            o_ref[rs, cs] = jax.nn.gelu(acc, approximate=True).astype(o_ref.dtype)
        else:
            piece = n * (tn // SUB_N) + c
            scale = jnp.where(piece < NQ // SUB_N, HD ** -0.5, 1.0).astype(F32)
            rotate = jnp.logical_and(i >= RC // tm, piece < (NQ + NKVD) // SUB_N)

            roped = (acc * cos_ref[rs, :] + pltpu.roll(acc, SUB_N - 16, 1) * sin_ref[0, rs, :]
                     + pltpu.roll(acc, 16, 1) * sin_ref[1, rs, :])
            o_ref[rs, cs] = (jnp.where(rotate, roped, acc) * scale).astype(o_ref.dtype)


def _proj(S, mods, norm_g, layer, w, slot, n_cols, *, kind="none", w_t=False, small_rows=0,
          cos=None, sin=None, ctx_cols=None, name="proj"):
    tm = TM_PROJ
    tn = max(t for t in TN_PROJ if n_cols % t == 0)
    nct = RC // tm
    mods, mlayer = mods
    once = pl.Buffered(1)
    dual = isinstance(S, tuple)
    if dual:
        row_specs = [pl.BlockSpec((tm, D), lambda i, n: (jnp.minimum(i, nct - 1), 0), pipeline_mode=once),
                     pl.BlockSpec((tm, D), lambda i, n: (jnp.maximum(i - nct, 0), 0))]
    else:
        row_specs = [pl.BlockSpec((tm, D), lambda i, n: (i, 0))]
    in_specs = row_specs + [
        pl.BlockSpec((1, 1, 6, D), lambda i, n: (mlayer, _mod_row(i, tm), 0, 0)),
        pl.BlockSpec((1, 4, D), lambda i, n: (layer, 0, 0)),
        (pl.BlockSpec((1, tn, D), lambda i, n: (slot, n, 0)) if w_t
         else pl.BlockSpec((1, D, tn), lambda i, n: (slot, 0, n))),
    ]
    args = (list(S) if dual else [S]) + [mods, norm_g, w]
    if kind == "rope":
        pos = lambda i, n: jnp.where(i < nct, 0, (i - nct) % (L // tm))
        in_specs += [pl.BlockSpec((tm, SUB_N), lambda i, n: (pos(i, n), 0)),
                     pl.BlockSpec((2, tm, SUB_N), lambda i, n: (0, pos(i, n), 0))]
        args += [cos, sin]
    out_specs = [pl.BlockSpec((tm, tn), lambda i, n: (i, n))]
    out_shape = [jax.ShapeDtypeStruct((R, n_cols), BF16)]
    if small_rows:
        assert w_t and n_cols % small_rows == 0
        in_specs.append(pl.BlockSpec((1, small_rows, D), lambda i, n: (slot, n_cols // small_rows, 0)))
        args.append(w)
        out_specs.append(pl.BlockSpec((tm, small_rows), lambda i, n: (i, 0)))
        out_shape.append(jax.ShapeDtypeStruct((R, small_rows), BF16))
    res = pl.pallas_call(
        functools.partial(_proj_body, kind, small_rows, w_t, dual, ctx_cols, tm, tn),
        grid=(R // tm, n_cols // tn),
        in_specs=in_specs,
        out_specs=out_specs,
        out_shape=out_shape,
        scratch_shapes=[pltpu.VMEM((tm, D), BF16), pltpu.VMEM((tm, 128), F32)],
        compiler_params=_cparams(("parallel", "arbitrary")),
        name=name,
    )(*args)
    return res if small_rows else res[0]


def _gla_constants():
    c = GCH
    r = np.arange(c)[:, None]
    t = np.arange(c)[None, :]
    blocks = [(t <= r)]
    masks = []
    for lv in range(G_LEVELS):
        s = c >> (lv + 1)
        m = r // s
        odd = (m % 2) == 1
        if s < 8:
            a_odd = (t > s * m) & (t <= r)
            a_even = (t > r) & (t <= s * (m + 1))
            blocks.append(np.where(odd, a_odd, a_even))
        masks.append(odd & ((t // s) == m - 1))
    masks.append(r == t)
    a_f = np.concatenate(blocks, axis=0).astype(np.float32)
    m_f = np.stack(masks).astype(np.float32)
    a_b = np.concatenate([blk[::-1, ::-1] for blk in blocks], axis=0).astype(np.float32)
    m_b = m_f[:, ::-1, ::-1]
    amat = np.stack([a_f, a_b])
    amat = np.concatenate([amat, amat], axis=2)
    return jnp.asarray(amat, BF16), jnp.asarray(np.stack([m_f, m_b]), F32)


def _gla_body(ctx_out, qf, kf, vf, af, qb, kb, vb, ab, wa_ref, ba_ref, a_ref, m_ref, of_ref, ob_ref, s_scr):
    c = pl.program_id(1)

    @pl.when(c == 0)
    def _():
        s_scr[...] = jnp.zeros_like(s_scr)

    row = lax.broadcasted_iota(jnp.int32, (GCH, GDK), 0)
    odd_rows = {s: ((row // s) % 2) == 1 for s in (4, 2, 1)}
    dirs = ((qf, kf, vf, af, of_ref, GCH - 1), (qb, kb, vb, ab, ob_ref, 0))

    def process(with_out):
        for d, (q_ref, k_ref, v_ref, ar_ref, o_ref, last) in enumerate(dirs):
            araw = _dot(ar_ref[...], wa_ref[d]) + ba_ref[d:d + 1, :]
            g = (jnp.minimum(araw, 0.0) - jnp.log1p(jnp.exp(-jnp.abs(araw)))) * (LOG2E / 16.0)
            g1 = g.astype(BF16)
            g2 = (g - g1.astype(F32)).astype(BF16)
            amat = a_ref[d] if with_out else a_ref[d, 0:GCH, :]
            ex_mm = _dot(amat, jnp.concatenate([g1, g2], axis=0))

            def level_exponent(bcum, ex_h, lv):
                s = GCH >> (lv + 1)
                if s < 8:
                    n_big = G_LEVELS - G_SMALL
                    return ex_h[(1 + lv - n_big) * GCH:(2 + lv - n_big) * GCH]
                parts = []
                for p in range(GCH // (2 * s)):
                    lo = 2 * p * s
                    ev, od = bcum[lo:lo + s], bcum[lo + s:lo + 2 * s]
                    if d == 0:
                        ref = bcum[lo + s:lo + s + 1]
                        parts += [ref - ev, od - ref]
                    else:
                        ref = bcum[lo + s - 1:lo + s]
                        parts += [ev - ref, ref - od]
                return jnp.concatenate(parts, axis=0)

            def zsel(q, k, lv):
                s = GCH >> (lv + 1)
                if s >= 8:
                    parts = [(q if (m % 2 == 1) == (d == 0) else k)[m * s:(m + 1) * s]
                             for m in range(GCH // s)]
                    return jnp.concatenate(parts, axis=0)
                return jnp.where(odd_rows[s], q, k) if d == 0 else jnp.where(odd_rows[s], k, q)

            if not with_out:
                o_ref[...] = jnp.zeros_like(o_ref)
            for h in range(GH):
                sl = slice(h * GDK, (h + 1) * GDK)
                vs = slice(h * GDV, (h + 1) * GDV)
                k = k_ref[:, sl].astype(F32)
                v = v_ref[:, vs]
                ex_h = ex_mm[:, sl]
                bcum = ex_h[0:GCH]
                e_cum = jnp.exp2(bcum)
                e_rem = jnp.exp2((bcum[GCH - 1:GCH] if d == 0 else bcum[0:1]) - bcum)
                st = s_scr[d, h]
                if with_out:
                    q = q_ref[:, sl].astype(F32) * (GDK ** -0.5)
                    o = _dot_nt((q * e_cum).astype(BF16), st.astype(BF16))
                    att = m_ref[d, G_LEVELS] * _dot_nt(q.astype(BF16), k.astype(BF16))
                    for lv in range(G_LEVELS):
                        z = (zsel(q, k, lv) * jnp.exp2(level_exponent(bcum, ex_h, lv))).astype(BF16)
                        att = att + m_ref[d, lv] * _dot_nt(z, z)
                    o = o + _dot(att.astype(BF16), v)
                    o_ref[:, vs] = o.astype(o_ref.dtype)
                s_scr[d, h] = st * e_cum[last:last + 1, :] + _dot_tn(v, (k * e_rem).astype(BF16))

    if ctx_out:
        process(True)
    else:
        pl.when(c >= GC_CH)(lambda: process(True))
        pl.when(c < GC_CH)(lambda: process(False))


def _gla_scan(p, a, wa2p, ba, slot, amat, masks, ctx_out):
    def rowblk(b, m):
        return jnp.where(m < GC_CH, b * GC_CH + m, RC // GCH + b * (L // GCH) + (m - GC_CH))

    def bwd(c):
        return jnp.where(c < GC_CH, GC_CH - 1 - c, GN_CH + GC_CH - 1 - c)

    hk = GH * GDK
    hv = GH * GDV

    def specs(order):
        return [
            pl.BlockSpec((GCH, hk), lambda b, c: (rowblk(b, order(c)), 0)),
            pl.BlockSpec((GCH, hk), lambda b, c: (rowblk(b, order(c)), 1)),
            pl.BlockSpec((GCH, hv), lambda b, c: (rowblk(b, order(c)), 1)),
            pl.BlockSpec((GCH, 2 * GRANK), lambda b, c: (rowblk(b, order(c)), 0)),
        ]

    in_specs = specs(lambda c: c) + specs(bwd) + [
        pl.BlockSpec((2, 2 * GRANK, hk), lambda b, c: (0, 0, 0)),
        pl.BlockSpec((1, 2, hk), lambda b, c: (slot, 0, 0)),
        pl.BlockSpec((2, G_ROWS, 2 * GCH), lambda b, c: (0, 0, 0)),
        pl.BlockSpec((2, G_LEVELS + 1, GCH, GCH), lambda b, c: (0, 0, 0, 0)),
    ]
    out_specs = [
        pl.BlockSpec((GCH, hv), lambda b, c: (rowblk(b, c), 0)),
        pl.BlockSpec((GCH, hv), lambda b, c: (rowblk(b, bwd(c)), 0)),
    ]

    def body(qf, kf, vf, af, qb, kb, vb, ab, wa_ref, ba_ref, a_ref, m_ref, of_ref, ob_ref, s_scr):
        _gla_body(ctx_out, qf, kf, vf, af, qb, kb, vb, ab, wa_ref, ba_ref.at[0], a_ref, m_ref, of_ref, ob_ref, s_scr)

    return pl.pallas_call(
        body,
        grid=(B, GN_CH),
        in_specs=in_specs,
        out_specs=out_specs,
        out_shape=[jax.ShapeDtypeStruct((R, hv), BF16)] * 2,
        scratch_shapes=[pltpu.VMEM((2, GH, GDV, GDK), F32)],
        compiler_params=_cparams(("parallel", "arbitrary")),
        name="gla_scan",
    )(p, p, p, a, p, p, p, a, wa2p, ba, amat, masks)


def _swa_body(sink_ref, q_ref, kc_ref, vc_ref, k0, k1, k2, v0, v1, v2, o_ref):
    j = pl.program_id(1)
    is_ctx = j < LC // QB
    jl = j - LC // QB
    lo = jnp.where(is_ctx, 0, jnp.where(jl >= 1, 0, QB))
    hi = jnp.where(is_ctx, 0, jnp.where(jl <= L // QB - 2, 3 * QB, 2 * QB))
    nk = LC + 3 * QB
    rr = lax.broadcasted_iota(jnp.int32, (QB, nk), 0)
    tt = lax.broadcasted_iota(jnp.int32, (QB, nk), 1) - LC
    dlt = tt - rr
    valid = (tt < 0) | ((dlt >= 0) & (dlt <= 2 * WIN) & (tt >= lo) & (tt < hi))
    bias = jnp.where(valid, 0.0, NEG_INF)

    kcat = jnp.concatenate([kc_ref[...], k0[...], k1[...], k2[...]], axis=0).astype(F32)
    vcat = jnp.concatenate([vc_ref[...], v0[...], v1[...], v2[...]], axis=0).astype(F32)
    lane_q = lax.broadcasted_iota(jnp.int32, (QB, 128), 1)

    def dup(xp, half):
        low = lax.broadcasted_iota(jnp.int32, xp.shape, 1) < HD
        rolled = pltpu.roll(xp, HD, 1)
        out = jnp.where(low, xp, rolled) if half == 0 else jnp.where(low, rolled, xp)
        return out.astype(BF16)

    for kh in range(NKV):
        ps = slice((kh // 2) * 128, (kh // 2 + 1) * 128)
        kd, vd = dup(kcat[:, ps], kh % 2), dup(vcat[:, ps], kh % 2)
        qs = []
        for g in range(GRP):
            h = kh * GRP + g
            blk = q_ref[:, (h // 2) * 128:(h // 2 + 1) * 128].astype(F32)
            keep = (lane_q < HD) if h % 2 == 0 else (lane_q >= HD)
            qs.append(jnp.where(keep, blk, 0.0).astype(BF16))
        qst = jnp.concatenate(qs, axis=0)
        logits = _dot_nt(qst, kd)
        es, invs = [], []
        for g in range(GRP):
            lg = logits[g * QB:(g + 1) * QB] + bias
            s = sink_ref[0, kh * GRP + g]
            m = jnp.maximum(jnp.max(lg, axis=-1, keepdims=True), s)
            e = jnp.exp(lg - m)
            den = jnp.sum(e, axis=-1, keepdims=True) + jnp.exp(s - m)
            es.append(e.astype(BF16))
            invs.append(1.0 / den)
        res = _dot(jnp.concatenate(es, axis=0), vd) * jnp.concatenate(invs, axis=0)
        for m2 in range(GRP // 2):
            even = res[(2 * m2) * QB:(2 * m2 + 1) * QB]
            odd = res[(2 * m2 + 1) * QB:(2 * m2 + 2) * QB]
            col = (kh * (GRP // 2) + m2) * 128
            o_ref[:, col:col + 128] = jnp.where(lane_q < HD, even, odd).astype(o_ref.dtype)


def _swa_attention(p, sink):
    ncb = LC // QB
    nlb = L // QB
    lat0 = RC // QB
    kcol = NQ // NKVD

    def qrow(b, j):
        return jnp.where(j < ncb, b * ncb + j, lat0 + b * nlb + (j - ncb))

    def wrow(off):
        def f(b, j):
            jl = jnp.clip(j - ncb + off, 0, nlb - 1)
            return lat0 + b * nlb + jl
        return f

    in_specs = [
        pl.BlockSpec(memory_space=pltpu.SMEM),
        pl.BlockSpec((QB, NQ), lambda b, j: (qrow(b, j), 0)),
        pl.BlockSpec((LC, NKVD), lambda b, j: (b, kcol)),
        pl.BlockSpec((LC, NKVD), lambda b, j: (b, kcol + 1)),
    ]
    for col in (kcol, kcol + 1):
        for off in (-1, 0, 1):
            in_specs.append(pl.BlockSpec((QB, NKVD), functools.partial(
                lambda b, j, f, cc: (f(b, j), cc), f=wrow(off), cc=col)))
    return pl.pallas_call(
        _swa_body,
        grid=(B, ncb + nlb),
        in_specs=in_specs,
        out_specs=pl.BlockSpec((QB, NQ), lambda b, j: (qrow(b, j), 0)),
        out_shape=jax.ShapeDtypeStruct((R, NQ), BF16),
        compiler_params=_cparams(("parallel", "parallel")),
        name="swa_attn",
    )(sink, p, p, p, p, p, p, p, p, p)


def _gla_prologue(rows, of_ref, ob_ref, og_ref, on_ref):
    o = of_ref[rows, :].astype(F32) + ob_ref[rows, :].astype(F32)
    parts = [_rms(o[:, h * GDV:(h + 1) * GDV], on_ref[0, :, h * GDV:(h + 1) * GDV]) for h in range(GH)]
    return (jnp.concatenate(parts, axis=-1) * _silu(og_ref[rows, :].astype(F32))).astype(BF16)


def _swa_prologue(rows, o_ref):
    return o_ref[rows, :]


def _gmlp_prologue(rows, u_ref, v_ref, lng_ref, lnb_ref, ws_ref, bsb_ref, lhs_scr):
    v = v_ref[rows, :].astype(F32)
    mu = jnp.mean(v, axis=-1, keepdims=True)
    var = jnp.mean(jnp.square(v - mu), axis=-1, keepdims=True)
    vn = ((v - mu) * lax.rsqrt(var + EPS) * lng_ref[0] + lnb_ref[0]).astype(BF16)
    for ch in range(v.shape[0] // MC):
        rs = slice(ch * MC, (ch + 1) * MC)
        rg = slice(rows.start + ch * MC, rows.start + (ch + 1) * MC)
        for g in range(MG):
            cs = slice(g * 128, (g + 1) * 128)
            mixed = _dot(ws_ref[0, g], vn[rs, cs]) + bsb_ref[:, cs]
            lhs_scr[rg, cs] = (u_ref[rg, cs].astype(F32) * mixed).astype(BF16)
    return lhs_scr[rows, :]


OUT_SPLIT = 2


def _out_body(prologue, n_in, dual, *refs):
    in_refs = refs[:n_in]
    rest = list(refs[n_in:])
    wo_ref = rest.pop(0)

    def tile(load_x):
        mod_ref, ng_ref, o_ref = rest[:3]
        piece = o_ref.shape[0] // OUT_SPLIT
        for r in range(OUT_SPLIT):
            rows = slice(r * piece, (r + 1) * piece)
            y = _dot(prologue(rows, *in_refs, *rest[3:]), wo_ref[0])
            o_ref[rows, :] = load_x(rows) + mod_ref[0, 0, 2:3, :] * _rms(y, ng_ref[0, 1:2, :])

    _with_row_source(dual, TM_OUT, rest, tile)()


def _out_proj(prologue, ins, in_specs, wo, slot, S, mods, norm_g, layer, *, skip_ctx, scratch=(),
              name="out_proj"):
    tm = TM_OUT
    mods, mlayer = mods
    off = RC // tm if skip_ctx else 0
    rows = RX if skip_ctx else R
    specs = [pl.BlockSpec(bs, functools.partial(lambda i, f: f(i + off), f=f)) for bs, f in in_specs]
    dual = isinstance(S, tuple)
    nct = RC // tm
    if dual:
        assert not skip_ctx
        row_specs = [pl.BlockSpec((tm, D), lambda i: (jnp.minimum(i, nct - 1), 0), pipeline_mode=pl.Buffered(1)),
                     pl.BlockSpec((tm, D), lambda i: (jnp.maximum(i - nct, 0), 0))]
    else:
        row_specs = [pl.BlockSpec((tm, D), lambda i: (i + off, 0))]
    specs += [pl.BlockSpec((1, D, D), lambda i: (slot, 0, 0), pipeline_mode=pl.Buffered(1))] + row_specs + [
        pl.BlockSpec((1, 1, 6, D), lambda i: (mlayer, _mod_row(i + off, tm), 0, 0)),
        pl.BlockSpec((1, 4, D), lambda i: (layer, 0, 0)),
    ]
    return pl.pallas_call(
        functools.partial(_out_body, prologue, len(ins), dual),
        grid=(rows // tm,),
        in_specs=specs,
        out_specs=pl.BlockSpec((tm, D), lambda i: (i, 0)),
        out_shape=jax.ShapeDtypeStruct((rows, D), F32),
        scratch_shapes=list(scratch),
        compiler_params=_cparams(("parallel",)),
        name=name,
    )(*ins, wo, *(S if dual else (S,)), mods, norm_g)


SIDE_TN = 256


def _ffn_body(nf, side, x_ref, mod_ref, ng_ref, w1_ref, w3_ref, w2_ref, *rest):
    f = pl.program_id(1)
    if side:
        c_ref, aw_ref, ab_ref, o_ref, mo_ref, h_scr, rs_scr = rest

        def side_job():
            mo_ref[0] = _dot(_silu(c_ref[...]).astype(BF16), aw_ref[0].astype(BF16)) + ab_ref[0]
    else:
        o_ref, h_scr, rs_scr = rest
        side_job = lambda: None

    @pl.when(f == 0)
    def _():
        _row_rsqrt(lambda rows: x_ref[rows, :], rs_scr)
        gsc = ng_ref[0, 2:3, :] * (1.0 + mod_ref[0, 0, 4:5, :])

        def pro(rows):
            h = x_ref[rows, :] * _lanes(rs_scr[rows, :], D) * gsc + mod_ref[0, 0, 3:4, :]
            h_scr[rows, :] = h.astype(BF16)
        _for_row_chunks(x_ref.shape[0], pro)

    def partial_sum():
        hb = h_scr[...]
        a = _dot(hb, w1_ref[0].astype(BF16))
        b = _dot(hb, w3_ref[0].astype(BF16))
        return _dot((_silu(a) * b).astype(BF16), w2_ref[0].astype(BF16))

    @pl.when(f == 0)
    def _():
        side_job()
        o_ref[...] = partial_sum()

    @pl.when(f > 0)
    def _():
        side_job()
        o_ref[...] += partial_sum()

    @pl.when(f == nf - 1)
    def _():
        _row_rsqrt(lambda rows: o_ref[rows, :], rs_scr)
        gg = mod_ref[0, 0, 5:6, :] * ng_ref[0, 3:4, :]

        def epi(rows):
            o_ref[rows, :] = x_ref[rows, :] + o_ref[rows, :] * _lanes(rs_scr[rows, :], D) * gg
        _for_row_chunks(x_ref.shape[0], epi)


def _ffn(S, mods, norm_g, w1, w3, w2, layer, adaln=None):
    tm, tf = TM_FFN, TF_FFN
    nf = FF // tf
    off = (R - S.shape[0]) // tm
    mods, mlayer = mods
    in_specs = [
        pl.BlockSpec((tm, D), lambda i, f: (i, 0)),
        pl.BlockSpec((1, 1, 6, D), lambda i, f: (mlayer, _mod_row(i + off, tm), 0, 0)),
        pl.BlockSpec((1, 4, D), lambda i, f: (layer, 0, 0)),
        pl.BlockSpec((1, D, tf), lambda i, f: (layer, 0, f)),
        pl.BlockSpec((1, D, tf), lambda i, f: (layer, 0, f)),
        pl.BlockSpec((1, tf, D), lambda i, f: (layer, f, 0)),
    ]
    args = [S, mods, norm_g, w1, w3, w2]
    out_specs = [pl.BlockSpec((tm, D), lambda i, f: (i, 0))]
    out_shape = [jax.ShapeDtypeStruct(S.shape, F32)]
    side = 0
    if adaln is not None:
        cvec, ada_w, ada_b = adaln
        n6 = 6 * D
        per_layer = n6 // SIDE_TN
        side = (DEPTH - 1) * per_layer
        assert side <= (S.shape[0] // tm) * nf

        def blk(i, f):
            t = jnp.minimum(i * nf + f, side - 1)
            return t // per_layer, t % per_layer

        in_specs += [
            pl.BlockSpec((8, D), lambda i, f: (0, 0)),
            pl.BlockSpec((1, D, SIDE_TN), lambda i, f: (1 + blk(i, f)[0], 0, blk(i, f)[1])),
            pl.BlockSpec((1, 1, SIDE_TN), lambda i, f: (1 + blk(i, f)[0], 0, blk(i, f)[1])),
        ]
        args += [cvec, ada_w, ada_b.reshape(DEPTH, 1, n6)]
        out_specs.append(pl.BlockSpec((1, 8, SIDE_TN), lambda i, f: (blk(i, f)[0], 0, blk(i, f)[1])))
        out_shape.append(jax.ShapeDtypeStruct((DEPTH - 1, 8, n6), F32))
    res = pl.pallas_call(
        functools.partial(_ffn_body, nf, side),
        grid=(S.shape[0] // tm, nf),
        in_specs=in_specs,
        out_specs=out_specs,
        out_shape=out_shape,
        scratch_shapes=[pltpu.VMEM((tm, D), BF16), pltpu.VMEM((tm, 128), F32)],
        compiler_params=_cparams(("arbitrary" if side else "parallel", "arbitrary")),
        name="ffn",
    )(*args)
    return res if adaln is not None else res[0]


def _rope_tables():
    quarter = HD // 4
    inv_freq = 10000.0 ** (-jnp.arange(quarter, dtype=F32) / quarter)
    row = jnp.repeat(jnp.arange(L // 64), 64).astype(F32)
    col = jnp.tile(jnp.arange(64), L // 64).astype(F32)
    ang_r = row[:, None] * inv_freq
    ang_c = col[:, None] * inv_freq
    ang = jnp.concatenate([ang_r, ang_r, ang_c, ang_c], axis=-1)
    reps = SUB_N // HD
    cos, sin = jnp.tile(jnp.cos(ang), (1, reps)), jnp.tile(jnp.sin(ang), (1, reps))
    first = (jnp.arange(SUB_N) % 32) < 16
    return cos, jnp.stack([jnp.where(first, -sin, 0.0), jnp.where(first, 0.0, sin)])


def kernel(x, c, ctx, c_ctx, ada_w, ada_b, norm_g, ffn_w1, ffn_w3, ffn_w2,
           gla_w_in, gla_wa2, gla_ba, gla_onorm_g, gla_wo,
           attn_w_in, attn_sink, attn_wo,
           gmlp_w_in, gmlp_ln_g, gmlp_ln_b, gmlp_ws, gmlp_bs, gmlp_wo):
    S = (ctx.reshape(RC, D), x.reshape(RX, D))
    cvec = jnp.concatenate([c_ctx[None, :], c, jnp.zeros((3, D), F32)], axis=0)
    mods0 = _modulation(cvec, ada_w, ada_b, 1).reshape(1, 8, 6, D)
    mods_rest = None
    rowtile = lambda i: (i, 0)
    w1, w3, w2 = ffn_w1, ffn_w3, ffn_w2
    gla_w, gla_wo_b = jnp.swapaxes(gla_w_in, 1, 2), gla_wo.astype(BF16)
    hk, hv = GH * GDK, GH * GDV

    for i in range(DEPTH):
        last = i == DEPTH - 1
        kind, slot = i % 3, i // 3
        mods = (mods0, 0) if i == 0 else (mods_rest, i - 1)
        if kind == 0:
            p, a = _proj(S, mods, norm_g, i, gla_w, slot, 2 * hk + 2 * hv, w_t=True, small_rows=2 * GRANK,
                         ctx_cols=(hk, 2 * hk + hv) if last else None, name="gla_proj")
            wa2 = gla_wa2[slot]
            wa2p = jnp.zeros((2, 2 * GRANK, hk), F32)
            wa2p = wa2p.at[0, :GRANK].set(wa2[0]).at[1, GRANK:].set(wa2[1]).astype(BF16)
            amat, masks = _gla_constants()
            o_f, o_b = _gla_scan(p, a, wa2p, gla_ba, slot, amat, masks, ctx_out=not last)
            ins = [o_f, o_b, p, gla_onorm_g.reshape(-1, 1, hv)]
            in_specs = [((TM_OUT, hv), rowtile), ((TM_OUT, hv), rowtile),
                        ((TM_OUT, hv), lambda i: (i, 2)),
                        ((1, 1, hv), functools.partial(lambda i, s: (s, 0, 0), s=slot))]
            S = _out_proj(_gla_prologue, ins, in_specs, gla_wo_b, slot, S, mods, norm_g, i,
                          skip_ctx=last, name="gla_out")
        elif kind == 1:
            cos, sin = _rope_tables()
            p = _proj(S, mods, norm_g, i, attn_w_in, slot, NQ + 2 * NKVD, kind="rope",
                      cos=cos, sin=sin, name="swa_proj")
            o = _swa_attention(p, attn_sink[slot:slot + 1])
            S = _out_proj(_swa_prologue, [o], [((TM_OUT, NQ), rowtile)], attn_wo.astype(BF16), slot,
                          S, mods, norm_g, i, skip_ctx=last, name="swa_out")
        else:
            p = _proj(S, mods, norm_g, i, gmlp_w_in, slot, 2 * D, kind="gelu",
                      name="gmlp_proj")
            bsb = jnp.repeat(gmlp_bs[slot].T, 128, axis=1)
            sel = functools.partial(lambda i, s: (s, 0, 0), s=slot)
            ins = [p, p, gmlp_ln_g.reshape(-1, 1, D), gmlp_ln_b.reshape(-1, 1, D),
                   gmlp_ws.astype(BF16), bsb]
            in_specs = [((TM_OUT, D), rowtile), ((TM_OUT, D), lambda i: (i, 1)),
                        ((1, 1, D), sel), ((1, 1, D), sel),
                        ((1, MG, MC, MC), functools.partial(lambda i, s: (s, 0, 0, 0), s=slot)),
                        ((MC, D), lambda i: (0, 0))]
            S = _out_proj(_gmlp_prologue, ins, in_specs, gmlp_wo.astype(BF16), slot, S, mods, norm_g, i,
                          skip_ctx=last, scratch=[pltpu.VMEM((TM_OUT, D), BF16)], name="gmlp_out")
        if i == 0:
            S, mods_rest = _ffn(S, mods, norm_g, w1, w3, w2, i, adaln=(cvec, ada_w, ada_b))
            mods_rest = mods_rest.reshape(DEPTH - 1, 8, 6, D)
        else:
            S = _ffn(S, mods, norm_g, w1, w3, w2, i)
    return S.reshape(B, L, D)
```

```python
import functools

import numpy as np
import jax
import jax.numpy as jnp
from jax import lax
from jax.experimental import pallas as pl
from jax.experimental.pallas import tpu as pltpu

F32 = jnp.float32
BF16 = jnp.bfloat16

D = 2048
B = 4
L = 2048
LC = 256
DEPTH = 4
FF = 5632
EPS = 1e-6
NEG_INF = -1e30
LOG2E = 1.4426950408889634

RC = B * LC
RX = B * L
R = RC + RX

GH = 4
GDK = 256
GDV = 512
GRANK = 16
GCH = 128
GN_CH = (LC + L) // GCH
GC_CH = LC // GCH
G_LEVELS = 7
G_SMALL = 3
G_ROWS = (1 + G_SMALL) * GCH

HD = 64
NH = 32
NKV = 4
GRP = 8
WIN = 128
QB = 128
SWA_STACK = 4
NQ = NH * HD
NKVD = NKV * HD

MC = 128
MG = 16

TM_PROJ = 1024
TN_PROJ = (1280, 1024, 512)
TM_OUT = 512
TM_FFN = 1024
TF_FFN = 256

VMEM_LIMIT = 56 * 1024 * 1024


def _cparams(sem, vmem=VMEM_LIMIT):
    return pltpu.CompilerParams(dimension_semantics=sem, vmem_limit_bytes=vmem)


def _mod_row(i, tm):
    nct = RC // tm
    return jnp.where(i < nct, 0, 1 + (i - nct) // (L // tm))


def _rms(y, g):
    ms = jnp.mean(y * y, axis=-1, keepdims=True)
    return y * lax.rsqrt(ms + EPS) * g


ROW_CHUNK = 32


def _for_row_chunks(n_rows, fn, unroll=4):
    def body(r, carry):
        fn(pl.ds(pl.multiple_of(r * ROW_CHUNK, ROW_CHUNK), ROW_CHUNK))
        return carry
    lax.fori_loop(0, n_rows // ROW_CHUNK, body, 0, unroll=unroll)


def _row_rsqrt(load, rs_scr):
    def body(rows):
        x = load(rows)
        ms = jnp.mean(x * x, axis=-1, keepdims=True)
        rs_scr[rows, :] = jnp.broadcast_to(lax.rsqrt(ms + EPS), (ROW_CHUNK, 128))
    _for_row_chunks(rs_scr.shape[0], body, unroll=8)


def _lanes(rs, width):
    return jnp.concatenate([rs] * (width // 128), axis=1)


def _silu(a):
    return a * jax.nn.sigmoid(a)


def _dot(a, b):
    return jnp.dot(a, b, preferred_element_type=F32)


def _dot_nt(a, b):
    return lax.dot_general(a, b, (((1,), (1,)), ((), ())), preferred_element_type=F32)


def _dot_tn(a, b):
    return lax.dot_general(a, b, (((0,), (0,)), ((), ())), preferred_element_type=F32)


def _mod_body(c_ref, w_ref, b_ref, o_ref):
    s = _silu(c_ref[...]).astype(BF16)
    o_ref[0] = _dot(s, w_ref[0].astype(BF16)) + b_ref[0]


def _modulation(cvec, ada_w, ada_b, n_layers):
    tn = 1024
    n6 = 6 * D
    return pl.pallas_call(
        _mod_body,
        grid=(n_layers, n6 // tn),
        in_specs=[
            pl.BlockSpec((8, D), lambda l, n: (0, 0)),
            pl.BlockSpec((1, D, tn), lambda l, n: (l, 0, n)),
            pl.BlockSpec((1, 1, tn), lambda l, n: (l, 0, n)),
        ],
        out_specs=pl.BlockSpec((1, 8, tn), lambda l, n: (l, 0, n)),
        out_shape=jax.ShapeDtypeStruct((n_layers, 8, n6), F32),
        compiler_params=_cparams(("parallel", "parallel")),
        name="adaln_mod",
    )(cvec, ada_w, ada_b.reshape(DEPTH, 1, n6))


SUB_N = 256
SUB_M = 256


def _with_row_source(dual, tile_rows, refs, fn):
    if not dual:
        x_ref = refs.pop(0)
        return lambda: fn(lambda rows: x_ref[rows, :])
    c_ref, x_ref = refs.pop(0), refs.pop(0)
    is_ctx = pl.program_id(0) < RC // tile_rows

    def run():
        pl.when(is_ctx)(lambda: fn(lambda rows: c_ref[rows, :]))
        pl.when(jnp.logical_not(is_ctx))(lambda: fn(lambda rows: x_ref[rows, :]))
    return run


def _proj_body(kind, small_rows, w_t, dual, ctx_cols, tm, tn, *refs):
    rest = list(refs)
    mod_ref, ng_ref = rest[2 if dual else 1], rest[3 if dual else 2]
    h_scr, rs_scr = rest[-2], rest[-1]

    def norm_rows(load_x):
        _row_rsqrt(load_x, rs_scr)
        gsc = ng_ref[0, 0:1, :] * (1.0 + mod_ref[0, 0, 1:2, :])

        def pro(rows):
            h = load_x(rows) * _lanes(rs_scr[rows, :], D) * gsc + mod_ref[0, 0, 0:1, :]
            h_scr[rows, :] = h.astype(BF16)
        _for_row_chunks(tm, pro)

    prologue = _with_row_source(dual, tm, rest, norm_rows)
    mod_ref, ng_ref, w_ref = rest.pop(0), rest.pop(0), rest.pop(0)
    if kind == "rope":
        cos_ref, sin_ref = rest.pop(0), rest.pop(0)
    if small_rows:
        ws_ref = rest.pop(0)
    o_ref = rest.pop(0)
    if small_rows:
        os_ref = rest.pop(0)
    h_scr, rs_scr = rest.pop(0), rest.pop(0)
    i = pl.program_id(0)
    n = pl.program_id(1)

    @pl.when(n == 0)
    def _():
        prologue()
        if small_rows:
            os_ref[...] = _dot_nt(h_scr[...], ws_ref[0].astype(BF16)).astype(os_ref.dtype)

    if ctx_cols is not None:
        lo, hi = ctx_cols
        needed = (i >= RC // tm) | (((n + 1) * tn > lo) & (n * tn < hi))
        pl.when(needed)(lambda: _proj_columns(kind, w_t, tm, tn, i, n, h_scr, w_ref, o_ref, None, None))

        @pl.when(jnp.logical_not(needed))
        def _():
            o_ref[...] = jnp.zeros_like(o_ref)
    else:
        _proj_columns(kind, w_t, tm, tn, i, n, h_scr, w_ref, o_ref,
                      cos_ref if kind == "rope" else None, sin_ref if kind == "rope" else None)


def _proj_columns(kind, w_t, tm, tn, i, n, h_scr, w_ref, o_ref, cos_ref, sin_ref):
    sub_m = SUB_M if kind == "rope" else tm
    for c, r in [(c, r) for c in range(tn // SUB_N) for r in range(tm // sub_m)]:
        cs = slice(c * SUB_N, (c + 1) * SUB_N)
        rs = slice(r * sub_m, (r + 1) * sub_m)
        hb = h_scr[rs, :]
        if w_t:
            acc = _dot_nt(hb, w_ref[0, cs, :].astype(BF16))
        else:
            acc = _dot(hb, w_ref[0, :, cs].astype(BF16))
        if kind == "none":
            o_ref[rs, cs] = acc.astype(o_ref.dtype)
        elif kind == "gelu":
            o_ref[rs, cs] = jax.nn.gelu(acc, approximate=True).astype(o_ref.dtype)
        else:
            piece = n * (tn // SUB_N) + c
            scale = jnp.where(piece < NQ // SUB_N, HD ** -0.5, 1.0).astype(F32)
            rotate = jnp.logical_and(i >= RC // tm, piece < (NQ + NKVD) // SUB_N)

            roped = (acc * cos_ref[rs, :] + pltpu.roll(acc, SUB_N - 16, 1) * sin_ref[0, rs, :]
                     + pltpu.roll(acc, 16, 1) * sin_ref[1, rs, :])
            o_ref[rs, cs] = (jnp.where(rotate, roped, acc) * scale).astype(o_ref.dtype)


def _proj(S, mods, norm_g, layer, w, slot, n_cols, *, kind="none", w_t=False, small_rows=0,
          cos=None, sin=None, ctx_cols=None, name="proj"):
    tm = TM_PROJ
    tn = max(t for t in TN_PROJ if n_cols % t == 0)
    nct = RC // tm
    mods, mlayer = mods
    once = pl.Buffered(1)
    dual = isinstance(S, tuple)
    if dual:
        row_specs = [pl.BlockSpec((tm, D), lambda i, n: (jnp.minimum(i, nct - 1), 0), pipeline_mode=once),
                     pl.BlockSpec((tm, D), lambda i, n: (jnp.maximum(i - nct, 0), 0))]
    else:
        row_specs = [pl.BlockSpec((tm, D), lambda i, n: (i, 0))]
    in_specs = row_specs + [
        pl.BlockSpec((1, 1, 6, D), lambda i, n: (mlayer, _mod_row(i, tm), 0, 0)),
        pl.BlockSpec((1, 4, D), lambda i, n: (layer, 0, 0)),
        (pl.BlockSpec((1, tn, D), lambda i, n: (slot, n, 0)) if w_t
         else pl.BlockSpec((1, D, tn), lambda i, n: (slot, 0, n))),
    ]
    args = (list(S) if dual else [S]) + [mods, norm_g, w]
    if kind == "rope":
        pos = lambda i, n: jnp.where(i < nct, 0, (i - nct) % (L // tm))
        in_specs += [pl.BlockSpec((tm, SUB_N), lambda i, n: (pos(i, n), 0)),
                     pl.BlockSpec((2, tm, SUB_N), lambda i, n: (0, pos(i, n), 0))]
        args += [cos, sin]
    out_specs = [pl.BlockSpec((tm, tn), lambda i, n: (i, n))]
    out_shape = [jax.ShapeDtypeStruct((R, n_cols), BF16)]
    if small_rows:
        assert w_t and n_cols % small_rows == 0
        in_specs.append(pl.BlockSpec((1, small_rows, D), lambda i, n: (slot, n_cols // small_rows, 0)))
        args.append(w)
        out_specs.append(pl.BlockSpec((tm, small_rows), lambda i, n: (i, 0)))
        out_shape.append(jax.ShapeDtypeStruct((R, small_rows), BF16))
    res = pl.pallas_call(
        functools.partial(_proj_body, kind, small_rows, w_t, dual, ctx_cols, tm, tn),
        grid=(R // tm, n_cols // tn),
        in_specs=in_specs,
        out_specs=out_specs,
        out_shape=out_shape,
        scratch_shapes=[pltpu.VMEM((tm, D), BF16), pltpu.VMEM((tm, 128), F32)],
        compiler_params=_cparams(("parallel", "arbitrary")),
        name=name,
    )(*args)
    return res if small_rows else res[0]


def _gla_constants():
    c = GCH
    r = np.arange(c)[:, None]
    t = np.arange(c)[None, :]
    blocks = [(t <= r)]
    masks = []
    for lv in range(G_LEVELS):
        s = c >> (lv + 1)
        m = r // s
        odd = (m % 2) == 1
        if s < 8:
            a_odd = (t > s * m) & (t <= r)
            a_even = (t > r) & (t <= s * (m + 1))
            blocks.append(np.where(odd, a_odd, a_even))
        masks.append(odd & ((t // s) == m - 1))
    masks.append(r == t)
    a_f = np.concatenate(blocks, axis=0).astype(np.float32)
    m_f = np.stack(masks).astype(np.float32)
    a_b = np.concatenate([blk[::-1, ::-1] for blk in blocks], axis=0).astype(np.float32)
    m_b = m_f[:, ::-1, ::-1]
    amat = np.stack([a_f, a_b])
    amat = np.concatenate([amat, amat], axis=2)
    return jnp.asarray(amat, BF16), jnp.asarray(np.stack([m_f, m_b]), F32)


def _gla_body(ctx_out, qf, kf, vf, af, qb, kb, vb, ab, wa_ref, ba_ref, a_ref, m_ref, of_ref, ob_ref, s_scr):
    c = pl.program_id(1)

    @pl.when(c == 0)
    def _():
        s_scr[...] = jnp.zeros_like(s_scr)

    row = lax.broadcasted_iota(jnp.int32, (GCH, GDK), 0)
    odd_rows = {s: ((row // s) % 2) == 1 for s in (4, 2, 1)}
    dirs = ((qf, kf, vf, af, of_ref, GCH - 1), (qb, kb, vb, ab, ob_ref, 0))

    def process(with_out):
        for d, (q_ref, k_ref, v_ref, ar_ref, o_ref, last) in enumerate(dirs):
            araw = _dot(ar_ref[...], wa_ref[d]) + ba_ref[d:d + 1, :]
            g = (jnp.minimum(araw, 0.0) - jnp.log1p(jnp.exp(-jnp.abs(araw)))) * (LOG2E / 16.0)
            g1 = g.astype(BF16)
            g2 = (g - g1.astype(F32)).astype(BF16)
            amat = a_ref[d] if with_out else a_ref[d, 0:GCH, :]
            ex_mm = _dot(amat, jnp.concatenate([g1, g2], axis=0))

            def level_exponent(bcum, ex_h, lv):
                s = GCH >> (lv + 1)
                if s < 8:
                    n_big = G_LEVELS - G_SMALL
                    return ex_h[(1 + lv - n_big) * GCH:(2 + lv - n_big) * GCH]
                parts = []
                for p in range(GCH // (2 * s)):
                    lo = 2 * p * s
                    ev, od = bcum[lo:lo + s], bcum[lo + s:lo + 2 * s]
                    if d == 0:
                        ref = bcum[lo + s:lo + s + 1]
                        parts += [ref - ev, od - ref]
                    else:
                        ref = bcum[lo + s - 1:lo + s]
                        parts += [ev - ref, ref - od]
                return jnp.concatenate(parts, axis=0)

            def zsel(q, k, lv):
                s = GCH >> (lv + 1)
                if s >= 8:
                    parts = [(q if (m % 2 == 1) == (d == 0) else k)[m * s:(m + 1) * s]
                             for m in range(GCH // s)]
                    return jnp.concatenate(parts, axis=0)
                return jnp.where(odd_rows[s], q, k) if d == 0 else jnp.where(odd_rows[s], k, q)

            if not with_out:
                o_ref[...] = jnp.zeros_like(o_ref)
            for h in range(GH):
                sl = slice(h * GDK, (h + 1) * GDK)
                vs = slice(h * GDV, (h + 1) * GDV)
                k = k_ref[:, sl].astype(F32)
                v = v_ref[:, vs]
                ex_h = ex_mm[:, sl]
                bcum = ex_h[0:GCH]
                e_cum = jnp.exp2(bcum)
                e_rem = jnp.exp2((bcum[GCH - 1:GCH] if d == 0 else bcum[0:1]) - bcum)
                st = s_scr[d, h]
                if with_out:
                    q = q_ref[:, sl].astype(F32) * (GDK ** -0.5)
                    o = _dot_nt((q * e_cum).astype(BF16), st.astype(BF16))
                    att = m_ref[d, G_LEVELS] * _dot_nt(q.astype(BF16), k.astype(BF16))
                    for lv in range(G_LEVELS):
                        z = (zsel(q, k, lv) * jnp.exp2(level_exponent(bcum, ex_h, lv))).astype(BF16)
                        att = att + m_ref[d, lv] * _dot_nt(z, z)
                    o = o + _dot(att.astype(BF16), v)
                    o_ref[:, vs] = o.astype(o_ref.dtype)
                s_scr[d, h] = st * e_cum[last:last + 1, :] + _dot_tn(v, (k * e_rem).astype(BF16))

    if ctx_out:
        process(True)
    else:
        pl.when(c >= GC_CH)(lambda: process(True))
        pl.when(c < GC_CH)(lambda: process(False))


def _gla_scan(p, a, wa2p, ba, slot, amat, masks, ctx_out):
    def rowblk(b, m):
        return jnp.where(m < GC_CH, b * GC_CH + m, RC // GCH + b * (L // GCH) + (m - GC_CH))

    def bwd(c):
        return jnp.where(c < GC_CH, GC_CH - 1 - c, GN_CH + GC_CH - 1 - c)

    hk = GH * GDK
    hv = GH * GDV

    def specs(order):
        return [
            pl.BlockSpec((GCH, hk), lambda b, c: (rowblk(b, order(c)), 0)),
            pl.BlockSpec((GCH, hk), lambda b, c: (rowblk(b, order(c)), 1)),
            pl.BlockSpec((GCH, hv), lambda b, c: (rowblk(b, order(c)), 1)),
            pl.BlockSpec((GCH, 2 * GRANK), lambda b, c: (rowblk(b, order(c)), 0)),
        ]

    in_specs = specs(lambda c: c) + specs(bwd) + [
        pl.BlockSpec((2, 2 * GRANK, hk), lambda b, c: (0, 0, 0)),
        pl.BlockSpec((1, 2, hk), lambda b, c: (slot, 0, 0)),
        pl.BlockSpec((2, G_ROWS, 2 * GCH), lambda b, c: (0, 0, 0)),
        pl.BlockSpec((2, G_LEVELS + 1, GCH, GCH), lambda b, c: (0, 0, 0, 0)),
    ]
    out_specs = [
        pl.BlockSpec((GCH, hv), lambda b, c: (rowblk(b, c), 0)),
        pl.BlockSpec((GCH, hv), lambda b, c: (rowblk(b, bwd(c)), 0)),
    ]

    def body(qf, kf, vf, af, qb, kb, vb, ab, wa_ref, ba_ref, a_ref, m_ref, of_ref, ob_ref, s_scr):
        _gla_body(ctx_out, qf, kf, vf, af, qb, kb, vb, ab, wa_ref, ba_ref.at[0], a_ref, m_ref, of_ref, ob_ref, s_scr)

    return pl.pallas_call(
        body,
        grid=(B, GN_CH),
        in_specs=in_specs,
        out_specs=out_specs,
        out_shape=[jax.ShapeDtypeStruct((R, hv), BF16)] * 2,
        scratch_shapes=[pltpu.VMEM((2, GH, GDV, GDK), F32)],
        compiler_params=_cparams(("parallel", "arbitrary")),
        name="gla_scan",
    )(p, p, p, a, p, p, p, a, wa2p, ba, amat, masks)


def _swa_body(sink_ref, q_ref, kc_ref, vc_ref, k0, k1, k2, v0, v1, v2, o_ref):
    j = pl.program_id(1)
    is_ctx = j < LC // QB
    jl = j - LC // QB
    lo = jnp.where(is_ctx, 0, jnp.where(jl >= 1, 0, QB))
    hi = jnp.where(is_ctx, 0, jnp.where(jl <= L // QB - 2, 3 * QB, 2 * QB))
    nk = LC + 3 * QB
    rr = lax.broadcasted_iota(jnp.int32, (QB, nk), 0)
    tt = lax.broadcasted_iota(jnp.int32, (QB, nk), 1) - LC
    dlt = tt - rr
    valid = (tt < 0) | ((dlt >= 0) & (dlt <= 2 * WIN) & (tt >= lo) & (tt < hi))
    bias = jnp.where(valid, 0.0, NEG_INF)

    kcat = jnp.concatenate([kc_ref[...], k0[...], k1[...], k2[...]], axis=0).astype(F32)
    vcat = jnp.concatenate([vc_ref[...], v0[...], v1[...], v2[...]], axis=0).astype(F32)
    lane_q = lax.broadcasted_iota(jnp.int32, (QB, 128), 1)

    def dup(xp, half):
        low = lax.broadcasted_iota(jnp.int32, xp.shape, 1) < HD
        rolled = pltpu.roll(xp, HD, 1)
        out = jnp.where(low, xp, rolled) if half == 0 else jnp.where(low, rolled, xp)
        return out.astype(BF16)

    for kh in range(NKV):
        ps = slice((kh // 2) * 128, (kh // 2 + 1) * 128)
        kd, vd = dup(kcat[:, ps], kh % 2), dup(vcat[:, ps], kh % 2)
        for g0 in range(0, GRP, SWA_STACK):
            qs = []
            for g in range(g0, g0 + SWA_STACK):
                h = kh * GRP + g
                blk = q_ref[:, (h // 2) * 128:(h // 2 + 1) * 128].astype(F32)
                keep = (lane_q < HD) if h % 2 == 0 else (lane_q >= HD)
                qs.append(jnp.where(keep, blk, 0.0).astype(BF16))
            logits = _dot_nt(jnp.concatenate(qs, axis=0), kd)
            es, invs = [], []
            for g in range(SWA_STACK):
                lg = logits[g * QB:(g + 1) * QB] + bias
                s = sink_ref[0, kh * GRP + g0 + g]
                m = jnp.maximum(jnp.max(lg, axis=-1, keepdims=True), s)
                e = jnp.exp(lg - m)
                den = jnp.sum(e, axis=-1, keepdims=True) + jnp.exp(s - m)
                es.append(e.astype(BF16))
                invs.append(1.0 / den)
            res = _dot(jnp.concatenate(es, axis=0), vd) * jnp.concatenate(invs, axis=0)
            for m2 in range(SWA_STACK // 2):
                even = res[(2 * m2) * QB:(2 * m2 + 1) * QB]
                odd = res[(2 * m2 + 1) * QB:(2 * m2 + 2) * QB]
                col = (kh * (GRP // 2) + g0 // 2 + m2) * 128
                o_ref[:, col:col + 128] = jnp.where(lane_q < HD, even, odd).astype(o_ref.dtype)


def _swa_attention(p, sink):
    ncb = LC // QB
    nlb = L // QB
    lat0 = RC // QB
    kcol = NQ // NKVD

    def qrow(b, j):
        return jnp.where(j < ncb, b * ncb + j, lat0 + b * nlb + (j - ncb))

    def wrow(off):
        def f(b, j):
            jl = jnp.clip(j - ncb + off, 0, nlb - 1)
            return lat0 + b * nlb + jl
        return f

    in_specs = [
        pl.BlockSpec(memory_space=pltpu.SMEM),
        pl.BlockSpec((QB, NQ), lambda b, j: (qrow(b, j), 0)),
        pl.BlockSpec((LC, NKVD), lambda b, j: (b, kcol)),
        pl.BlockSpec((LC, NKVD), lambda b, j: (b, kcol + 1)),
    ]
    for col in (kcol, kcol + 1):
        for off in (-1, 0, 1):
            in_specs.append(pl.BlockSpec((QB, NKVD), functools.partial(
                lambda b, j, f, cc: (f(b, j), cc), f=wrow(off), cc=col)))
    return pl.pallas_call(
        _swa_body,
        grid=(B, ncb + nlb),
        in_specs=in_specs,
        out_specs=pl.BlockSpec((QB, NQ), lambda b, j: (qrow(b, j), 0)),
        out_shape=jax.ShapeDtypeStruct((R, NQ), BF16),
        compiler_params=_cparams(("parallel", "parallel")),
        name="swa_attn",
    )(sink, p, p, p, p, p, p, p, p, p)


def _gla_prologue(rows, of_ref, ob_ref, og_ref, on_ref):
    o = of_ref[rows, :].astype(F32) + ob_ref[rows, :].astype(F32)
    parts = [_rms(o[:, h * GDV:(h + 1) * GDV], on_ref[0, :, h * GDV:(h + 1) * GDV]) for h in range(GH)]
    return (jnp.concatenate(parts, axis=-1) * _silu(og_ref[rows, :].astype(F32))).astype(BF16)


def _swa_prologue(rows, o_ref):
    return o_ref[rows, :]


def _gmlp_prologue(rows, u_ref, v_ref, lng_ref, lnb_ref, ws_ref, bsb_ref, lhs_scr):
    v = v_ref[rows, :].astype(F32)
    mu = jnp.mean(v, axis=-1, keepdims=True)
    var = jnp.mean(jnp.square(v - mu), axis=-1, keepdims=True)
    vn = ((v - mu) * lax.rsqrt(var + EPS) * lng_ref[0] + lnb_ref[0]).astype(BF16)
    for ch in range(v.shape[0] // MC):
        rs = slice(ch * MC, (ch + 1) * MC)
        rg = slice(rows.start + ch * MC, rows.start + (ch + 1) * MC)
        for g in range(MG):
            cs = slice(g * 128, (g + 1) * 128)
            mixed = _dot(ws_ref[0, g], vn[rs, cs]) + bsb_ref[:, cs]
            lhs_scr[rg, cs] = (u_ref[rg, cs].astype(F32) * mixed).astype(BF16)
    return lhs_scr[rows, :]


OUT_SPLIT = 2


def _out_body(prologue, n_in, dual, *refs):
    in_refs = refs[:n_in]
    rest = list(refs[n_in:])
    wo_ref = rest.pop(0)

    def tile(load_x):
        mod_ref, ng_ref, o_ref = rest[:3]
        piece = o_ref.shape[0] // OUT_SPLIT
        for r in range(OUT_SPLIT):
            rows = slice(r * piece, (r + 1) * piece)
            y = _dot(prologue(rows, *in_refs, *rest[3:]), wo_ref[0])
            o_ref[rows, :] = load_x(rows) + mod_ref[0, 0, 2:3, :] * _rms(y, ng_ref[0, 1:2, :])

    _with_row_source(dual, TM_OUT, rest, tile)()


def _out_proj(prologue, ins, in_specs, wo, slot, S, mods, norm_g, layer, *, skip_ctx, scratch=(),
              name="out_proj"):
    tm = TM_OUT
    mods, mlayer = mods
    off = RC // tm if skip_ctx else 0
    rows = RX if skip_ctx else R
    specs = [pl.BlockSpec(bs, functools.partial(lambda i, f: f(i + off), f=f)) for bs, f in in_specs]
    dual = isinstance(S, tuple)
    nct = RC // tm
    if dual:
        assert not skip_ctx
        row_specs = [pl.BlockSpec((tm, D), lambda i: (jnp.minimum(i, nct - 1), 0), pipeline_mode=pl.Buffered(1)),
                     pl.BlockSpec((tm, D), lambda i: (jnp.maximum(i - nct, 0), 0))]
    else:
        row_specs = [pl.BlockSpec((tm, D), lambda i: (i + off, 0))]
    specs += [pl.BlockSpec((1, D, D), lambda i: (slot, 0, 0), pipeline_mode=pl.Buffered(1))] + row_specs + [
        pl.BlockSpec((1, 1, 6, D), lambda i: (mlayer, _mod_row(i + off, tm), 0, 0)),
        pl.BlockSpec((1, 4, D), lambda i: (layer, 0, 0)),
    ]
    return pl.pallas_call(
        functools.partial(_out_body, prologue, len(ins), dual),
        grid=(rows // tm,),
        in_specs=specs,
        out_specs=pl.BlockSpec((tm, D), lambda i: (i, 0)),
        out_shape=jax.ShapeDtypeStruct((rows, D), F32),
        scratch_shapes=list(scratch),
        compiler_params=_cparams(("parallel",)),
        name=name,
    )(*ins, wo, *(S if dual else (S,)), mods, norm_g)


SIDE_TN = 256


def _ffn_body(nf, side, x_ref, mod_ref, ng_ref, w1_ref, w3_ref, w2_ref, *rest):
    f = pl.program_id(1)
    if side:
        c_ref, aw_ref, ab_ref, o_ref, mo_ref, h_scr, rs_scr = rest

        def side_job():
            mo_ref[0] = _dot(_silu(c_ref[...]).astype(BF16), aw_ref[0].astype(BF16)) + ab_ref[0]
    else:
        o_ref, h_scr, rs_scr = rest
        side_job = lambda: None

    @pl.when(f == 0)
    def _():
        _row_rsqrt(lambda rows: x_ref[rows, :], rs_scr)
        gsc = ng_ref[0, 2:3, :] * (1.0 + mod_ref[0, 0, 4:5, :])

        def pro(rows):
            h = x_ref[rows, :] * _lanes(rs_scr[rows, :], D) * gsc + mod_ref[0, 0, 3:4, :]
            h_scr[rows, :] = h.astype(BF16)
        _for_row_chunks(x_ref.shape[0], pro)

    def partial_sum():
        hb = h_scr[...]
        a = _dot(hb, w1_ref[0].astype(BF16))
        b = _dot(hb, w3_ref[0].astype(BF16))
        return _dot((_silu(a) * b).astype(BF16), w2_ref[0].astype(BF16))

    @pl.when(f == 0)
    def _():
        side_job()
        o_ref[...] = partial_sum()

    @pl.when(f > 0)
    def _():
        side_job()
        o_ref[...] += partial_sum()

    @pl.when(f == nf - 1)
    def _():
        _row_rsqrt(lambda rows: o_ref[rows, :], rs_scr)
        gg = mod_ref[0, 0, 5:6, :] * ng_ref[0, 3:4, :]

        def epi(rows):
            o_ref[rows, :] = x_ref[rows, :] + o_ref[rows, :] * _lanes(rs_scr[rows, :], D) * gg
        _for_row_chunks(x_ref.shape[0], epi)


def _ffn(S, mods, norm_g, w1, w3, w2, layer, adaln=None):
    tm, tf = TM_FFN, TF_FFN
    nf = FF // tf
    off = (R - S.shape[0]) // tm
    mods, mlayer = mods
    in_specs = [
        pl.BlockSpec((tm, D), lambda i, f: (i, 0)),
        pl.BlockSpec((1, 1, 6, D), lambda i, f: (mlayer, _mod_row(i + off, tm), 0, 0)),
        pl.BlockSpec((1, 4, D), lambda i, f: (layer, 0, 0)),
        pl.BlockSpec((1, D, tf), lambda i, f: (layer, 0, f)),
        pl.BlockSpec((1, D, tf), lambda i, f: (layer, 0, f)),
        pl.BlockSpec((1, tf, D), lambda i, f: (layer, f, 0)),
    ]
    args = [S, mods, norm_g, w1, w3, w2]
    out_specs = [pl.BlockSpec((tm, D), lambda i, f: (i, 0))]
    out_shape = [jax.ShapeDtypeStruct(S.shape, F32)]
    side = 0
    if adaln is not None:
        cvec, ada_w, ada_b = adaln
        n6 = 6 * D
        per_layer = n6 // SIDE_TN
        side = (DEPTH - 1) * per_layer
        assert side <= (S.shape[0] // tm) * nf

        def blk(i, f):
            t = jnp.minimum(i * nf + f, side - 1)
            return t // per_layer, t % per_layer

        in_specs += [
            pl.BlockSpec((8, D), lambda i, f: (0, 0)),
            pl.BlockSpec((1, D, SIDE_TN), lambda i, f: (1 + blk(i, f)[0], 0, blk(i, f)[1])),
            pl.BlockSpec((1, 1, SIDE_TN), lambda i, f: (1 + blk(i, f)[0], 0, blk(i, f)[1])),
        ]
        args += [cvec, ada_w, ada_b.reshape(DEPTH, 1, n6)]
        out_specs.append(pl.BlockSpec((1, 8, SIDE_TN), lambda i, f: (blk(i, f)[0], 0, blk(i, f)[1])))
        out_shape.append(jax.ShapeDtypeStruct((DEPTH - 1, 8, n6), F32))
    res = pl.pallas_call(
        functools.partial(_ffn_body, nf, side),
        grid=(S.shape[0] // tm, nf),
        in_specs=in_specs,
        out_specs=out_specs,
        out_shape=out_shape,
        scratch_shapes=[pltpu.VMEM((tm, D), BF16), pltpu.VMEM((tm, 128), F32)],
        compiler_params=_cparams(("arbitrary" if side else "parallel", "arbitrary")),
        name="ffn",
    )(*args)
    return res if adaln is not None else res[0]


def _rope_tables():
    quarter = HD // 4
    inv_freq = 10000.0 ** (-jnp.arange(quarter, dtype=F32) / quarter)
    row = jnp.repeat(jnp.arange(L // 64), 64).astype(F32)
    col = jnp.tile(jnp.arange(64), L // 64).astype(F32)
    ang_r = row[:, None] * inv_freq
    ang_c = col[:, None] * inv_freq
    ang = jnp.concatenate([ang_r, ang_r, ang_c, ang_c], axis=-1)
    reps = SUB_N // HD
    cos, sin = jnp.tile(jnp.cos(ang), (1, reps)), jnp.tile(jnp.sin(ang), (1, reps))
    first = (jnp.arange(SUB_N) % 32) < 16
    return cos, jnp.stack([jnp.where(first, -sin, 0.0), jnp.where(first, 0.0, sin)])


def kernel(x, c, ctx, c_ctx, ada_w, ada_b, norm_g, ffn_w1, ffn_w3, ffn_w2,
           gla_w_in, gla_wa2, gla_ba, gla_onorm_g, gla_wo,
           attn_w_in, attn_sink, attn_wo,
           gmlp_w_in, gmlp_ln_g, gmlp_ln_b, gmlp_ws, gmlp_bs, gmlp_wo):
    S = (ctx.reshape(RC, D), x.reshape(RX, D))
    cvec = jnp.concatenate([c_ctx[None, :], c, jnp.zeros((3, D), F32)], axis=0)
    mods0 = _modulation(cvec, ada_w, ada_b, 1).reshape(1, 8, 6, D)
    mods_rest = None
    rowtile = lambda i: (i, 0)
    w1, w3, w2 = ffn_w1, ffn_w3, ffn_w2
    gla_w, gla_wo_b = jnp.swapaxes(gla_w_in, 1, 2), gla_wo.astype(BF16)
    hk, hv = GH * GDK, GH * GDV

    for i in range(DEPTH):
        last = i == DEPTH - 1
        kind, slot = i % 3, i // 3
        mods = (mods0, 0) if i == 0 else (mods_rest, i - 1)
        if kind == 0:
            p, a = _proj(S, mods, norm_g, i, gla_w, slot, 2 * hk + 2 * hv, w_t=True, small_rows=2 * GRANK,
                         ctx_cols=(hk, 2 * hk + hv) if last else None, name="gla_proj")
            wa2 = gla_wa2[slot]
            wa2p = jnp.zeros((2, 2 * GRANK, hk), F32)
            wa2p = wa2p.at[0, :GRANK].set(wa2[0]).at[1, GRANK:].set(wa2[1]).astype(BF16)
            amat, masks = _gla_constants()
            o_f, o_b = _gla_scan(p, a, wa2p, gla_ba, slot, amat, masks, ctx_out=not last)
            ins = [o_f, o_b, p, gla_onorm_g.reshape(-1, 1, hv)]
            in_specs = [((TM_OUT, hv), rowtile), ((TM_OUT, hv), rowtile),
                        ((TM_OUT, hv), lambda i: (i, 2)),
                        ((1, 1, hv), functools.partial(lambda i, s: (s, 0, 0), s=slot))]
            S = _out_proj(_gla_prologue, ins, in_specs, gla_wo_b, slot, S, mods, norm_g, i,
                          skip_ctx=last, name="gla_out")
        elif kind == 1:
            cos, sin = _rope_tables()
            p = _proj(S, mods, norm_g, i, attn_w_in, slot, NQ + 2 * NKVD, kind="rope",
                      cos=cos, sin=sin, name="swa_proj")
            o = _swa_attention(p, attn_sink[slot:slot + 1])
            S = _out_proj(_swa_prologue, [o], [((TM_OUT, NQ), rowtile)], attn_wo.astype(BF16), slot,
                          S, mods, norm_g, i, skip_ctx=last, name="swa_out")
        else:
            p = _proj(S, mods, norm_g, i, gmlp_w_in, slot, 2 * D, kind="gelu",
                      name="gmlp_proj")
            bsb = jnp.repeat(gmlp_bs[slot].T, 128, axis=1)
            sel = functools.partial(lambda i, s: (s, 0, 0), s=slot)
            ins = [p, p, gmlp_ln_g.reshape(-1, 1, D), gmlp_ln_b.reshape(-1, 1, D),
                   gmlp_ws.astype(BF16), bsb]
            in_specs = [((TM_OUT, D), rowtile), ((TM_OUT, D), lambda i: (i, 1)),
                        ((1, 1, D), sel), ((1, 1, D), sel),
                        ((1, MG, MC, MC), functools.partial(lambda i, s: (s, 0, 0, 0), s=slot)),
                        ((MC, D), lambda i: (0, 0))]
            S = _out_proj(_gmlp_prologue, ins, in_specs, gmlp_wo.astype(BF16), slot, S, mods, norm_g, i,
                          skip_ctx=last, scratch=[pltpu.VMEM((TM_OUT, D), BF16)], name="gmlp_out")
        if i == 0:
            S, mods_rest = _ffn(S, mods, norm_g, w1, w3, w2, i, adaln=(cvec, ada_w, ada_b))
            mods_rest = mods_rest.reshape(DEPTH - 1, 8, 6, D)
        else:
            S = _ffn(S, mods, norm_g, w1, w3, w2, i)
    return S.reshape(B, L, D)
```
